```python
import jax, jax.numpy as jnp
from jax import lax
import numpy as np

D_MODEL = 1024
BATCH = 32
SEQ = 2048
DEPTH = 4
DEC_BATCH = 32
DEC_SEQ = 32
PAST_LEN = 1024

CHUNK = 64
PAST_CHUNKS = 8
BAND_PAST = PAST_CHUNKS * CHUNK
BAND = BAND_PAST + CHUNK
HEAD_DIM = 64
N_HEADS = D_MODEL // HEAD_DIM
REL_CLIP = 128
CONV_W = 31
D_FF = ((-(-8 * D_MODEL // 3)) + 255) // 256 * 256
N_ATTN = (DEPTH + 1) // 2
N_CONV = DEPTH // 2
EPS = 1e-6
NEG_INF = -1e30

kernel_name = "streaming_attn_conformer_hybrid_step"


def rms_norm(x, g):
    xf = x.astype(jnp.float32)
    y = xf * lax.rsqrt(jnp.mean(xf * xf, axis=-1, keepdims=True) + EPS)
    return (y * g.astype(jnp.float32)).astype(x.dtype)


def layer_norm(x, g, b):
    xf = x.astype(jnp.float32)
    xc = xf - jnp.mean(xf, axis=-1, keepdims=True)
    y = xc * lax.rsqrt(jnp.mean(xc * xc, axis=-1, keepdims=True) + EPS)
    return (y * g.astype(jnp.float32) + b.astype(jnp.float32)).astype(x.dtype)


def rel_bias(table, q_pos, k_pos):
    idx = jnp.clip(q_pos[:, None] - k_pos[None, :], -REL_CLIP, REL_CLIP) + REL_CLIP
    return table.astype(jnp.float32)[:, idx]


def attend(q, k, v, bias, valid):
    s = jnp.einsum('bqhd,bkhd->bhqk', q, k).astype(jnp.float32) * (HEAD_DIM ** -0.5) + bias[None]
    s = jnp.where(valid[None, None, None, :], s, NEG_INF)
    p = jax.nn.softmax(s, axis=-1).astype(v.dtype)
    return jnp.einsum('bhqk,bkhd->bqhd', p, v)


def qkv_proj(h, w_qkv, g_q, g_k):
    b, l, _ = h.shape
    qkv = (h @ w_qkv).reshape(b, l, 3, N_HEADS, HEAD_DIM)
    q = rms_norm(qkv[:, :, 0], g_q)
    k = rms_norm(qkv[:, :, 1], g_k)
    return q, k, qkv[:, :, 2]


def attn_prompt(h, w_qkv, g_q, g_k, w_o, table):
    b, l, _ = h.shape
    q, k, v = qkv_proj(h, w_qkv, g_q, g_k)
    pad = ((0, 0), (BAND_PAST, 0), (0, 0), (0, 0))
    k_pad = jnp.pad(k, pad)
    v_pad = jnp.pad(v, pad)
    bias = rel_bias(table, BAND_PAST + jnp.arange(CHUNK), jnp.arange(BAND))

    def one_chunk(c):
        start = c * CHUNK
        q_c = lax.dynamic_slice_in_dim(q, start, CHUNK, axis=1)
        k_c = lax.dynamic_slice_in_dim(k_pad, start, BAND, axis=1)
        v_c = lax.dynamic_slice_in_dim(v_pad, start, BAND, axis=1)
        valid = (start - BAND_PAST + jnp.arange(BAND)) >= 0
        return attend(q_c, k_c, v_c, bias, valid)

    o = lax.map(one_chunk, jnp.arange(l // CHUNK))
    o = jnp.moveaxis(o, 0, 1).reshape(b, l, D_MODEL)
    keep = min(BAND_PAST, l)
    return o @ w_o, k[:, l - keep:], v[:, l - keep:]


def attn_sample(h, ck, cv, w_qkv, g_q, g_k, w_o, table):
    b, l, _ = h.shape
    q, k, v = qkv_proj(h, w_qkv, g_q, g_k)
    n_past = ck.shape[1]
    k_all = jnp.concatenate([ck.astype(k.dtype), k], axis=1)
    v_all = jnp.concatenate([cv.astype(v.dtype), v], axis=1)
    q_pos = PAST_LEN + jnp.arange(l)
    k_pos = jnp.concatenate([PAST_LEN - n_past + jnp.arange(n_past), q_pos])
    bias = rel_bias(table, q_pos, k_pos)
    o = attend(q, k_all, v_all, bias, k_pos >= 0).reshape(b, l, D_MODEL)
    return o @ w_o, k, v


def conv_pre(h, pw1_w, pw1_b):
    a = h @ pw1_w + pw1_b
    return a[..., :D_MODEL] * jax.nn.sigmoid(a[..., D_MODEL:])


def conv_post(ctx, dw_w, dw_b, ln_g, ln_b, pw2_w, pw2_b):
    y = lax.conv_general_dilated(ctx, dw_w[:, None, :].astype(ctx.dtype), (1,), 'VALID',
                                 dimension_numbers=('NWC', 'WIO', 'NWC'),
                                 feature_group_count=D_MODEL) + dw_b
    y = jax.nn.silu(layer_norm(y, ln_g, ln_b))
    return y @ pw2_w + pw2_b


def swiglu(h, w_in, w_out):
    gu = h @ w_in
    return (jax.nn.silu(gu[..., :D_FF]) * gu[..., D_FF:]) @ w_out


def setup_inputs(seed: int = 0) -> dict:
    key = jax.random.key(seed)
    ks = jax.random.split(key, 24)

    def nrm(k, shape, scale):
        return jax.random.normal(k, shape, jnp.float32) * scale

    kv_len = min(BAND_PAST, PAST_LEN)
    return {
        "x_prompt": nrm(ks[0], (BATCH, SEQ, D_MODEL), 1.0),
        "x_sample": nrm(ks[1], (DEC_BATCH, DEC_SEQ, D_MODEL), 1.0),
        "cache_k": nrm(ks[2], (N_ATTN, DEC_BATCH, kv_len, N_HEADS, HEAD_DIM), 1.0),
        "cache_v": nrm(ks[3], (N_ATTN, DEC_BATCH, kv_len, N_HEADS, HEAD_DIM), 1.0),
        "state_conv": nrm(ks[4], (N_CONV, DEC_BATCH, CONV_W - 1, D_MODEL), 0.5),
        "norm_mix": 1.0 + nrm(ks[5], (DEPTH, D_MODEL), 0.01),
        "norm_ffn": 1.0 + nrm(ks[6], (DEPTH, D_MODEL), 0.01),
        "w_qkv": nrm(ks[7], (N_ATTN, D_MODEL, 3 * D_MODEL), D_MODEL ** -0.5),
        "q_norm": 1.0 + nrm(ks[8], (N_ATTN, HEAD_DIM), 0.01),
        "k_norm": 1.0 + nrm(ks[9], (N_ATTN, HEAD_DIM), 0.01),
        "rel_table": nrm(ks[10], (N_ATTN, N_HEADS, 2 * REL_CLIP + 1), 0.1),
        "w_o": nrm(ks[11], (N_ATTN, D_MODEL, D_MODEL), D_MODEL ** -0.5),
        "pw1_w": nrm(ks[12], (N_CONV, D_MODEL, 2 * D_MODEL), D_MODEL ** -0.5),
        "pw1_b": nrm(ks[13], (N_CONV, 2 * D_MODEL), 0.01),
        "dw_w": nrm(ks[14], (N_CONV, CONV_W, D_MODEL), CONV_W ** -0.5),
        "dw_b": nrm(ks[15], (N_CONV, D_MODEL), 0.01),
        "conv_ln_g": 1.0 + nrm(ks[16], (N_CONV, D_MODEL), 0.01),
        "conv_ln_b": nrm(ks[17], (N_CONV, D_MODEL), 0.01),
        "pw2_w": nrm(ks[18], (N_CONV, D_MODEL, D_MODEL), D_MODEL ** -0.5),
        "pw2_b": nrm(ks[19], (N_CONV, D_MODEL), 0.01),
        "ffn_w_in": nrm(ks[20], (DEPTH, D_MODEL, 2 * D_FF), D_MODEL ** -0.5),
        "ffn_w_out": nrm(ks[21], (DEPTH, D_FF, D_MODEL), D_FF ** -0.5),
    }


def reference(x_prompt, x_sample, cache_k, cache_v, state_conv, norm_mix, norm_ffn,
              w_qkv, q_norm, k_norm, rel_table, w_o, pw1_w, pw1_b, dw_w, dw_b,
              conv_ln_g, conv_ln_b, pw2_w, pw2_b, ffn_w_in, ffn_w_out):
    xp, xs = x_prompt, x_sample
    kp_new, vp_new, ks_new, vs_new, cp_new, cs_new = [], [], [], [], [], []
    for layer in range(DEPTH):
        hp = rms_norm(xp, norm_mix[layer])
        hs = rms_norm(xs, norm_mix[layer])
        if layer % 2 == 0:
            a = layer // 2
            mp, kp, vp = attn_prompt(hp, w_qkv[a], q_norm[a], k_norm[a], w_o[a], rel_table[a])
            ms, kss, vss = attn_sample(hs, cache_k[a], cache_v[a], w_qkv[a], q_norm[a],
                                       k_norm[a], w_o[a], rel_table[a])
            kp_new.append(kp)
            vp_new.append(vp)
            ks_new.append(kss)
            vs_new.append(vss)
        else:
            c = layer // 2
            up = conv_pre(hp, pw1_w[c], pw1_b[c])
            us = conv_pre(hs, pw1_w[c], pw1_b[c])
            ctx_p = jnp.pad(up, ((0, 0), (CONV_W - 1, 0), (0, 0)))
            ctx_s = jnp.concatenate([state_conv[c].astype(us.dtype), us], axis=1)
            mp = conv_post(ctx_p, dw_w[c], dw_b[c], conv_ln_g[c], conv_ln_b[c], pw2_w[c], pw2_b[c])
            ms = conv_post(ctx_s, dw_w[c], dw_b[c], conv_ln_g[c], conv_ln_b[c], pw2_w[c], pw2_b[c])
            cp_new.append(up[:, -(CONV_W - 1):])
            cs_new.append(ctx_s[:, -(CONV_W - 1):])
        xp = xp + mp
        xs = xs + ms
        xp = xp + swiglu(rms_norm(xp, norm_ffn[layer]), ffn_w_in[layer], ffn_w_out[layer])
        xs = xs + swiglu(rms_norm(xs, norm_ffn[layer]), ffn_w_in[layer], ffn_w_out[layer])
    return (xp, xs, jnp.stack(kp_new), jnp.stack(vp_new), jnp.stack(ks_new),
            jnp.stack(vs_new), jnp.stack(cp_new), jnp.stack(cs_new))
```

```python
import functools

import jax
import jax.numpy as jnp
from jax import lax
from jax.experimental import pallas as pl
from jax.experimental.pallas import tpu as pltpu

CHUNK = 64
PAST_CHUNKS = 8
BAND_PAST = PAST_CHUNKS * CHUNK
EPS = 1e-6
NEG_INF = -1e30

LANES = 128
MXU_DIM = 256
VMEM_LIMIT_BYTES = 56 * 1024 * 1024

ROW_TILE = 512
Q_TILE = 256
CONV_ROW_BLOCK = 64
CONV_ROW_STRIDE = 2
HALO_ROWS = 32

BF16 = jnp.bfloat16
F32 = jnp.float32


def _dot(a, b):
    return jnp.dot(a, b, preferred_element_type=F32)


def _dot_nt(a, b):
    return lax.dot_general(a, b, (((1,), (1,)), ((), ())), preferred_element_type=F32)


def _rms(x, g):
    ms = jnp.mean(x * x, axis=-1, keepdims=True)
    return x * lax.rsqrt(ms + EPS) * g


def _sigmoid(x):
    return 1.0 / (1.0 + jnp.exp(-x))


def _resident(shape):
    nd = len(shape)
    return pl.BlockSpec(shape, lambda *_: (0,) * nd, pipeline_mode=pl.Buffered(1))


def _params(sem):
    return pltpu.CompilerParams(dimension_semantics=sem, vmem_limit_bytes=VMEM_LIMIT_BYTES)


def _ffn_kernel(x_ref, g_ref, win_ref, wout_ref, o_ref, *, d_ff, n_chunks):
    x = x_ref[...]
    h = _rms(x, g_ref[...]).astype(BF16)
    cw = d_ff // n_chunks
    acc = None
    for c in range(n_chunks):
        gate = _dot(h, win_ref[:, c * cw:(c + 1) * cw])
        up = _dot(h, win_ref[:, d_ff + c * cw:d_ff + (c + 1) * cw])
        a = (gate * _sigmoid(gate) * up).astype(BF16)
        d = _dot(a, wout_ref[c * cw:(c + 1) * cw, :])
        acc = d if acc is None else acc + d
    o_ref[...] = x + acc


def _ffn_call(x, g, w_in, w_out):
    r, d = x.shape
    d_ff = w_out.shape[0]
    n_chunks = 2 if d_ff % (2 * LANES) == 0 else 1
    return pl.pallas_call(
        functools.partial(_ffn_kernel, d_ff=d_ff, n_chunks=n_chunks),
        grid=(r // ROW_TILE,),
        in_specs=[pl.BlockSpec((ROW_TILE, d), lambda i: (i, 0)),
                  _resident((1, d)), _resident(w_in.shape), _resident(w_out.shape)],
        out_specs=pl.BlockSpec((ROW_TILE, d), lambda i: (i, 0)),
        out_shape=jax.ShapeDtypeStruct((r, d), F32),
        input_output_aliases={0: 0},
        compiler_params=_params(("arbitrary",)),
        name="ffn",
    )(x, g, w_in, w_out)


def _qkv_kernel(x_ref, g_ref, w_ref, gq_ref, gk_ref, gmat_ref, q_ref, k_ref, v_ref, kf_ref, vf_ref,
                *, tiles_per_batch, n_prompt_tiles):
    i = pl.program_id(0)
    d = x_ref.shape[1]
    h = _rms(x_ref[...], g_ref[...]).astype(BF16)
    qkv = _dot(h, w_ref[...])
    gmat = gmat_ref[...]

    def head_norm(t, g):
        sq = (t * t).astype(BF16)
        ms = jnp.concatenate([_dot(sq[:, j * MXU_DIM:(j + 1) * MXU_DIM], gmat)
                              for j in range(d // MXU_DIM)], axis=1)
        return t * lax.rsqrt(ms + EPS) * g

    qn = head_norm(qkv[:, :d], gq_ref[...])
    kn = head_norm(qkv[:, d:2 * d], gk_ref[...])
    v = qkv[:, 2 * d:]
    for p in range(d // LANES):
        sl = slice(p * LANES, (p + 1) * LANES)
        q_ref[p] = qn[:, sl].astype(BF16)
        k_ref[p] = kn[:, sl].astype(BF16)
        v_ref[p] = v[:, sl].astype(BF16)

    keep = jnp.logical_or(i >= n_prompt_tiles, i % tiles_per_batch == tiles_per_batch - 1)

    @pl.when(keep)
    def _():
        kf_ref[...] = kn
        vf_ref[...] = v


def _qkv_call(x, g, w, gq, gk, gmat, *, n_batch, tiles_per_batch):
    r, d = x.shape
    n_tiles = r // ROW_TILE
    n_prompt_tiles = n_batch * tiles_per_batch
    n_keep = n_batch + n_tiles - n_prompt_tiles
    n_pairs = d // LANES

    def keep_map(i):
        return (jnp.where(i < n_prompt_tiles, i // tiles_per_batch, n_batch + i - n_prompt_tiles), 0)

    pm_spec = pl.BlockSpec((n_pairs, ROW_TILE, LANES), lambda i: (0, i, 0))
    pm_shape = jax.ShapeDtypeStruct((n_pairs, r, LANES), BF16)
    kf_shape = jax.ShapeDtypeStruct((n_keep * ROW_TILE, d), F32)
    return pl.pallas_call(
        functools.partial(_qkv_kernel, tiles_per_batch=tiles_per_batch, n_prompt_tiles=n_prompt_tiles),
        grid=(n_tiles,),
        in_specs=[pl.BlockSpec((ROW_TILE, d), lambda i: (i, 0)),
                  _resident((1, d)), _resident(w.shape), _resident((1, d)), _resident((1, d)),
                  _resident(gmat.shape)],
        out_specs=[pm_spec, pm_spec, pm_spec,
                   pl.BlockSpec((ROW_TILE, d), keep_map), pl.BlockSpec((ROW_TILE, d), keep_map)],
        out_shape=[pm_shape, pm_shape, pm_shape, kf_shape, kf_shape],
        compiler_params=_params(("arbitrary",)),
        name="qkv",
    )(x, g, w, gq, gk, gmat)


def _bias_kernel(r_ref, o_ref):
    n_keys = o_ref.shape[2]
    width = r_ref.shape[2]
    x = jnp.broadcast_to(r_ref[0], (Q_TILE, width))
    row = lax.broadcasted_iota(jnp.int32, (Q_TILE, width), 0)
    shift = 1
    while shift < Q_TILE:
        x = jnp.where((row & shift) != 0, pltpu.roll(x, shift, 1), x)
        shift *= 2
    x = x[:, :n_keys]
    qc = lax.broadcasted_iota(jnp.int32, (Q_TILE, n_keys), 0) // CHUNK
    kc = lax.broadcasted_iota(jnp.int32, (Q_TILE, n_keys), 1) // CHUNK
    in_band = jnp.logical_and(kc >= qc, kc <= qc + PAST_CHUNKS)
    o_ref[0] = jnp.where(in_band, x, NEG_INF)


def _bias_call(rel_table):
    n_heads, n_rel = rel_table.shape
    clip = (n_rel - 1) // 2
    n_keys = Q_TILE + BAND_PAST
    width = 1024
    assert width >= n_keys + Q_TILE and BAND_PAST >= clip
    far = jnp.broadcast_to(rel_table[:, n_rel - 1:], (n_heads, BAND_PAST - clip))
    near = jnp.broadcast_to(rel_table[:, :1], (n_heads, n_keys - (BAND_PAST + clip) - 1))
    wrap = jnp.broadcast_to(rel_table[:, n_rel - 1:], (n_heads, width - n_keys))
    gen = jnp.concatenate([far, rel_table[:, ::-1], near, wrap], axis=1).reshape(n_heads, 1, width)
    return pl.pallas_call(
        _bias_kernel,
        grid=(n_heads,),
        in_specs=[pl.BlockSpec((1, 1, width), lambda h: (h, 0, 0))],
        out_specs=pl.BlockSpec((1, Q_TILE, n_keys), lambda h: (h, 0, 0)),
        out_shape=jax.ShapeDtypeStruct((n_heads, Q_TILE, n_keys), F32),
        compiler_params=_params(("arbitrary",)),
        name="rel_bias",
    )(gen)


def _attn_kernel(q_ref, k0_ref, k1_ref, k2_ref, v0_ref, v1_ref, v2_ref, bias_ref, x_ref, wo_ref,
                 o_ref, opair_ref):
    t = pl.program_id(1)
    n_pairs = q_ref.shape[0]
    pen0 = jnp.where(t < 2, NEG_INF, 0.0).astype(F32)
    pen1 = jnp.where(t < 1, NEG_INF, 0.0).astype(F32)
    lo = lax.broadcasted_iota(jnp.int32, (Q_TILE, LANES), 1) < LANES // 2

    def pair_body(p, carry):
        qp = q_ref[p]
        zero = jnp.zeros_like(qp)
        qs = jnp.concatenate([jnp.where(lo, qp, zero), jnp.where(lo, zero, qp)], axis=0)
        s0 = _dot_nt(qs, k0_ref[p]) + bias_ref[p, :, 0:Q_TILE]
        s1 = _dot_nt(qs, k1_ref[p]) + bias_ref[p, :, Q_TILE:2 * Q_TILE]
        s2 = _dot_nt(qs, k2_ref[p]) + bias_ref[p, :, 2 * Q_TILE:3 * Q_TILE]
        m = jnp.maximum(jnp.maximum(jnp.max(s0, axis=1, keepdims=True) + pen0,
                                    jnp.max(s1, axis=1, keepdims=True) + pen1),
                        jnp.max(s2, axis=1, keepdims=True))
        e0 = jnp.exp(s0 - (m - pen0))
        e1 = jnp.exp(s1 - (m - pen1))
        e2 = jnp.exp(s2 - m)
        denom = (jnp.sum(e0, axis=1, keepdims=True) + jnp.sum(e1, axis=1, keepdims=True)
                 + jnp.sum(e2, axis=1, keepdims=True))
        o = (_dot(e0.astype(BF16), v0_ref[p]) + _dot(e1.astype(BF16), v1_ref[p])
             + _dot(e2.astype(BF16), v2_ref[p]))
        o = o * (1.0 / denom)
        opair_ref[p] = jnp.where(lo, o[:Q_TILE], o[Q_TILE:]).astype(BF16)
        return carry

    lax.fori_loop(0, n_pairs, pair_body, 0)
    o_all = jnp.concatenate([opair_ref[p] for p in range(n_pairs)], axis=1)
    o_ref[...] = x_ref[...] + _dot(o_all, wo_ref[...])


def _attn_call(x, q, k, v, bias2, w_o, *, n_batch, seq):
    r, d = x.shape
    n_pairs = q.shape[0]
    nt = seq // Q_TILE

    def blk(back):
        return pl.BlockSpec((n_pairs, Q_TILE, LANES),
                            lambda b, t: (0, b * nt + jnp.maximum(t - back, 0), 0))

    row_spec = pl.BlockSpec((Q_TILE, d), lambda b, t: (b * nt + t, 0))
    return pl.pallas_call(
        _attn_kernel,
        grid=(n_batch, nt),
        in_specs=[blk(0), blk(2), blk(1), blk(0), blk(2), blk(1), blk(0),
                  _resident(bias2.shape), row_spec, _resident(w_o.shape)],
        out_specs=row_spec,
        out_shape=jax.ShapeDtypeStruct((r, d), F32),
        scratch_shapes=[pltpu.VMEM((n_pairs, Q_TILE, LANES), BF16)],
        input_output_aliases={8: 0},
        compiler_params=_params(("arbitrary", "arbitrary")),
        name="attn_prompt",
    )(q, k, k, k, v, v, v, bias2, x, w_o)


def _attn_sample_kernel(q_ref, k_ref, v_ref, ck_ref, cv_ref, bias_ref, x_ref, wo_ref, o_ref):
    n_pairs, n_new, _ = q_ref.shape
    n_past = ck_ref.shape[0]
    lo = lax.broadcasted_iota(jnp.int32, (n_new, LANES), 1) < LANES // 2
    ck = ck_ref[...].astype(BF16)
    cv = cv_ref[...].astype(BF16)
    outs = []
    for p in range(n_pairs):
        sl = slice(p * LANES, (p + 1) * LANES)
        qp = q_ref[p]
        zero = jnp.zeros_like(qp)
        qs = jnp.concatenate([jnp.where(lo, qp, zero), jnp.where(lo, zero, qp)], axis=0)
        b_lo = bias_ref[p, 0:n_new, :]
        b_hi = bias_ref[p, Q_TILE:Q_TILE + n_new, :]
        bias = jnp.concatenate([b_lo, b_hi], axis=0)
        sc = _dot_nt(qs, ck[:, sl]) + bias[:, 0:n_past]
        sn = _dot_nt(qs, k_ref[p]) + bias[:, n_past:n_past + n_new]
        m = jnp.maximum(jnp.max(sc, axis=1, keepdims=True), jnp.max(sn, axis=1, keepdims=True))
        ec = jnp.exp(sc - m)
        en = jnp.exp(sn - m)
        denom = jnp.sum(ec, axis=1, keepdims=True) + jnp.sum(en, axis=1, keepdims=True)
        o = _dot(ec.astype(BF16), cv[:, sl]) + _dot(en.astype(BF16), v_ref[p])
        o = o * (1.0 / denom)
        outs.append(jnp.where(lo, o[:n_new], o[n_new:]).astype(BF16))
    o_all = jnp.concatenate(outs, axis=1)
    o_ref[...] = x_ref[...] + _dot(o_all, wo_ref[...])


def _attn_sample_call(x, q, k, v, cache_k, cache_v, layer, bias2, w_o, *, n_dec, n_new, row0):
    r, d = x.shape
    n_pairs = q.shape[0]
    n_past = cache_k.shape[2]
    blk0 = row0 // n_new
    pm_spec = pl.BlockSpec((n_pairs, n_new, LANES), lambda b: (0, blk0 + b, 0))
    cache_spec = pl.BlockSpec((None, None, n_past, d), lambda b: (layer, b, 0, 0))
    row_spec = pl.BlockSpec((n_new, d), lambda b: (blk0 + b, 0))
    return pl.pallas_call(
        _attn_sample_kernel,
        grid=(n_dec,),
        in_specs=[pm_spec, pm_spec, pm_spec, cache_spec, cache_spec,
                  _resident(bias2.shape), row_spec, _resident(w_o.shape)],
        out_specs=row_spec,
        out_shape=jax.ShapeDtypeStruct((r, d), F32),
        input_output_aliases={6: 0},
        compiler_params=_params(("arbitrary",)),
        name="attn_sample",
    )(q, k, v, cache_k, cache_v, bias2, x, w_o)


def _glu_kernel(x_ref, g_ref, w_ref, b_ref, u_ref):
    d = x_ref.shape[1]
    h = _rms(x_ref[...], g_ref[...]).astype(BF16)
    a = _dot(h, w_ref[...]) + b_ref[...]
    u_ref[...] = a[:, :d] * _sigmoid(a[:, d:])


def _glu_call(x, g, w, b):
    r, d = x.shape
    return pl.pallas_call(
        _glu_kernel,
        grid=(r // ROW_TILE,),
        in_specs=[pl.BlockSpec((ROW_TILE, d), lambda i: (i, 0)),
                  _resident((1, d)), _resident(w.shape), _resident((1, 2 * d))],
        out_specs=pl.BlockSpec((ROW_TILE, d), lambda i: (i, 0)),
        out_shape=jax.ShapeDtypeStruct((r, d), F32),
        compiler_params=_params(("arbitrary",)),
        name="conv_glu",
    )(x, g, w, b)


def _conv_tail(ctx_ref, y_ref, x, dw_ref, dwb_ref, lng_ref, lnb_ref, w2_ref, b2_ref):
    n_slabs, n_rows, _ = y_ref.shape
    conv_w = dw_ref.shape[0]
    first = HALO_ROWS - (conv_w - 1)
    rb = min(CONV_ROW_BLOCK, n_rows)
    per_phase = rb // CONV_ROW_STRIDE

    def row_block(i, carry):
        r0 = pl.multiple_of(i * rb, rb)
        for c in range(n_slabs):
            sl = slice(c * LANES, (c + 1) * LANES)
            win = ctx_ref.at[c, pl.ds(r0, rb + HALO_ROWS), :]
            out = y_ref.at[c, pl.ds(r0, rb), :]
            for ph in range(CONV_ROW_STRIDE):
                acc = jnp.zeros((per_phase, LANES), F32)
                for k in range(conv_w):
                    acc = acc + dw_ref[k:k + 1, sl] * win[pl.ds(first + k + ph, per_phase, stride=CONV_ROW_STRIDE), :]
                out[pl.ds(ph, per_phase, stride=CONV_ROW_STRIDE), :] = acc + dwb_ref[:, sl]
        return carry

    lax.fori_loop(0, n_rows // rb, row_block, 0)
    y = jnp.concatenate([y_ref[c] for c in range(n_slabs)], axis=1)
    yc = y - jnp.mean(y, axis=-1, keepdims=True)
    yn = yc * lax.rsqrt(jnp.mean(yc * yc, axis=-1, keepdims=True) + EPS) * lng_ref[...] + lnb_ref[...]
    act = (yn * _sigmoid(yn)).astype(BF16)
    return x + _dot(act, w2_ref[...]) + b2_ref[...]


def _fill_ctx(ctx_ref, halo, u_ref):
    for c in range(ctx_ref.shape[0]):
        sl = slice(c * LANES, (c + 1) * LANES)
        ctx_ref[c, 0:HALO_ROWS, :] = halo[:, sl]
        ctx_ref[c, HALO_ROWS:, :] = u_ref[:, sl]


def _conv_kernel(halo_ref, u_ref, x_ref, dw_ref, dwb_ref, lng_ref, lnb_ref, w2_ref, b2_ref, o_ref,
                 ctx_ref, y_ref):
    t = pl.program_id(1)
    halo = halo_ref[...]
    _fill_ctx(ctx_ref, jnp.where(t > 0, halo, jnp.zeros_like(halo)), u_ref)
    o_ref[...] = _conv_tail(ctx_ref, y_ref, x_ref[...], dw_ref, dwb_ref, lng_ref, lnb_ref, w2_ref, b2_ref)


def _conv_scratch(n_rows, d):
    return [pltpu.VMEM((d // LANES, HALO_ROWS + n_rows, LANES), F32), pltpu.VMEM((d // LANES, n_rows, LANES), F32)]


def _conv_call(x, u, dw, dwb, lng, lnb, w2, b2, *, n_batch, seq):
    r, d = x.shape
    nt = seq // ROW_TILE
    per_halo = ROW_TILE // HALO_ROWS
    row_spec = pl.BlockSpec((ROW_TILE, d), lambda b, t: (b * nt + t, 0))
    halo_spec = pl.BlockSpec((HALO_ROWS, d), lambda b, t: (jnp.maximum((b * nt + t) * per_halo - 1, 0), 0))
    return pl.pallas_call(
        _conv_kernel,
        grid=(n_batch, nt),
        in_specs=[halo_spec, row_spec, row_spec, _resident(dw.shape), _resident((1, d)), _resident((1, d)),
                  _resident((1, d)), _resident(w2.shape), _resident((1, d))],
        out_specs=row_spec,
        out_shape=jax.ShapeDtypeStruct((r, d), F32),
        scratch_shapes=_conv_scratch(ROW_TILE, d),
        input_output_aliases={2: 0},
        compiler_params=_params(("arbitrary", "arbitrary")),
        name="conv_prompt",
    )(u, u, x, dw, dwb, lng, lnb, w2, b2)


def _conv_sample_kernel(state_ref, u_ref, x_ref, dw_ref, dwb_ref, lng_ref, lnb_ref, w2_ref, b2_ref, o_ref,
                        ctx_ref, y_ref):
    _fill_ctx(ctx_ref, state_ref[...], u_ref)
    o_ref[...] = _conv_tail(ctx_ref, y_ref, x_ref[...], dw_ref, dwb_ref, lng_ref, lnb_ref, w2_ref, b2_ref)


def _conv_sample_call(x, u, state, layer, dw, dwb, lng, lnb, w2, b2, *, n_dec, n_new, row0):
    r, d = x.shape
    blk0 = row0 // n_new
    row_spec = pl.BlockSpec((n_new, d), lambda b: (blk0 + b, 0))
    state_spec = pl.BlockSpec((None, None, HALO_ROWS, d), lambda b: (layer, b, 0, 0))
    return pl.pallas_call(
        _conv_sample_kernel,
        grid=(n_dec,),
        in_specs=[state_spec, row_spec, row_spec, _resident(dw.shape), _resident((1, d)), _resident((1, d)),
                  _resident((1, d)), _resident(w2.shape), _resident((1, d))],
        out_specs=row_spec,
        out_shape=jax.ShapeDtypeStruct((r, d), F32),
        scratch_shapes=_conv_scratch(n_new, d),
        input_output_aliases={2: 0},
        compiler_params=_params(("arbitrary",)),
        name="conv_sample",
    )(state, u, x, dw, dwb, lng, lnb, w2, b2)


def kernel(x_prompt, x_sample, cache_k, cache_v, state_conv, norm_mix, norm_ffn, w_qkv, q_norm, k_norm,
           rel_table, w_o, pw1_w, pw1_b, dw_w, dw_b, conv_ln_g, conv_ln_b, pw2_w, pw2_b, ffn_w_in, ffn_w_out):
    n_batch, seq, d = x_prompt.shape
    n_dec, n_new, _ = x_sample.shape
    depth = norm_mix.shape[0]
    head_dim = q_norm.shape[1]
    n_heads = d // head_dim
    conv_w = dw_w.shape[1]
    n_prompt = n_batch * seq
    n_sample = n_dec * n_new
    keep = min(BAND_PAST, seq)
    assert seq % ROW_TILE == 0 and n_sample % ROW_TILE == 0 and keep == ROW_TILE
    assert LANES == 2 * head_dim and d % MXU_DIM == 0 and n_new % 8 == 0
    assert conv_w - 1 <= HALO_ROWS and n_new >= conv_w - 1
    assert cache_k.shape[2] == BAND_PAST and cache_k.shape[1] == n_dec

    x = jnp.concatenate([x_prompt.reshape(n_prompt, d), x_sample.reshape(n_sample, d)], axis=0)

    blk = jnp.arange(MXU_DIM) // head_dim
    gmat = jnp.where(blk[:, None] == blk[None, :], 1.0 / head_dim, 0.0).astype(BF16)
    inv_sqrt_dh = float(head_dim) ** -0.5
    cache_k4 = cache_k.reshape(cache_k.shape[0], n_dec, BAND_PAST, d)
    cache_v4 = cache_v.reshape(cache_v.shape[0], n_dec, BAND_PAST, d)
    state_pad = jnp.pad(state_conv, ((0, 0), (0, 0), (HALO_ROWS - (conv_w - 1), 0), (0, 0)))

    kp_new, vp_new, ks_new, vs_new, cp_new, cs_new = [], [], [], [], [], []
    for layer in range(depth):
        g_mix = norm_mix[layer].reshape(1, d)
        if layer % 2 == 0:
            a = layer // 2
            gq = (jnp.tile(q_norm[a], n_heads) * inv_sqrt_dh).reshape(1, d)
            gk = jnp.tile(k_norm[a], n_heads).reshape(1, d)
            q, k, v, kf, vf = _qkv_call(x, g_mix, w_qkv[a].astype(BF16), gq, gk, gmat,
                                        n_batch=n_batch, tiles_per_batch=seq // ROW_TILE)
            bias = _bias_call(rel_table[a])
            bias2 = bias.reshape(n_heads // 2, 2 * Q_TILE, Q_TILE + BAND_PAST)
            wo = w_o[a].astype(BF16)
            x = _attn_call(x, q, k, v, bias2, wo, n_batch=n_batch, seq=seq)
            x = _attn_sample_call(x, q, k, v, cache_k4, cache_v4, a, bias2, wo,
                                  n_dec=n_dec, n_new=n_new, row0=n_prompt)
            kp_new.append(kf[:n_batch * keep].reshape(n_batch, keep, n_heads, head_dim))
            vp_new.append(vf[:n_batch * keep].reshape(n_batch, keep, n_heads, head_dim))
            ks_new.append(kf[n_batch * keep:].reshape(n_dec, n_new, n_heads, head_dim))
            vs_new.append(vf[n_batch * keep:].reshape(n_dec, n_new, n_heads, head_dim))
        else:
            c = layer // 2
            u = _glu_call(x, g_mix, pw1_w[c].astype(BF16), pw1_b[c].reshape(1, 2 * d))
            conv_args = (dw_w[c], dw_b[c].reshape(1, d), conv_ln_g[c].reshape(1, d), conv_ln_b[c].reshape(1, d),
                         pw2_w[c].astype(BF16), pw2_b[c].reshape(1, d))
            x = _conv_call(x, u, *conv_args, n_batch=n_batch, seq=seq)
            x = _conv_sample_call(x, u, state_pad, c, *conv_args, n_dec=n_dec, n_new=n_new, row0=n_prompt)
            up = u[:n_prompt].reshape(n_batch, seq, d)
            us = u[n_prompt:].reshape(n_dec, n_new, d)
            cp_new.append(up[:, seq - (conv_w - 1):])
            cs_new.append(us[:, n_new - (conv_w - 1):])
        x = _ffn_call(x, norm_ffn[layer].reshape(1, d), ffn_w_in[layer].astype(BF16),
                      ffn_w_out[layer].astype(BF16))

    return (x[:n_prompt].reshape(n_batch, seq, d), x[n_prompt:].reshape(n_dec, n_new, d),
            jnp.stack(kp_new), jnp.stack(vp_new), jnp.stack(ks_new), jnp.stack(vs_new),
            jnp.stack(cp_new), jnp.stack(cs_new))
```

```python
import functools

import jax
import jax.numpy as jnp
from jax import lax
from jax.experimental import pallas as pl
from jax.experimental.pallas import tpu as pltpu

CHUNK = 64
PAST_CHUNKS = 8
BAND_PAST = PAST_CHUNKS * CHUNK
EPS = 1e-6
NEG_INF = -1e30

LANES = 128
MXU_DIM = 256
VMEM_LIMIT_BYTES = 56 * 1024 * 1024

ROW_TILE = 512
Q_TILE = 256
ATTN_PAIR_UNROLL = 8
CONV_ROW_BLOCK = 64
CONV_ROW_STRIDE = 2
HALO_ROWS = 32

BF16 = jnp.bfloat16
F32 = jnp.float32


def _dot(a, b):
    return jnp.dot(a, b, preferred_element_type=F32)


def _dot_nt(a, b):
    return lax.dot_general(a, b, (((1,), (1,)), ((), ())), preferred_element_type=F32)


def _rms(x, g):
    ms = jnp.mean(x * x, axis=-1, keepdims=True)
    return x * lax.rsqrt(ms + EPS) * g


def _sigmoid(x):
    return 1.0 / (1.0 + jnp.exp(-x))


def _fold(op, x):
    acc = x[:, 0:LANES]
    for j in range(1, x.shape[1] // LANES):
        acc = op(acc, x[:, j * LANES:(j + 1) * LANES])
    return acc


def _resident(shape):
    nd = len(shape)
    return pl.BlockSpec(shape, lambda *_: (0,) * nd, pipeline_mode=pl.Buffered(1))


def _params(sem):
    return pltpu.CompilerParams(dimension_semantics=sem, vmem_limit_bytes=VMEM_LIMIT_BYTES)


def _split_specs(np_t, d):
    return (pl.BlockSpec((ROW_TILE, d), lambda i: (jnp.minimum(i, np_t - 1), 0)),
            pl.BlockSpec((ROW_TILE, d), lambda i: (jnp.maximum(i - np_t, 0), 0)))


def _load_rows(i, np_t, xp_ref, xs_ref):
    return jnp.where(i < np_t, xp_ref[...], xs_ref[...])


def _store_rows(i, np_t, y, op_ref, os_ref):
    @pl.when(i < np_t)
    def _():
        op_ref[...] = y

    @pl.when(i >= np_t)
    def _():
        os_ref[...] = y


def _ffn_kernel(xp_ref, xs_ref, g_ref, win_ref, wout_ref, op_ref, os_ref, *, np_t, d_ff, n_chunks):
    i = pl.program_id(0)
    x = _load_rows(i, np_t, xp_ref, xs_ref)
    h = _rms(x, g_ref[...]).astype(BF16)
    cw = d_ff // n_chunks
    acc = None
    for c in range(n_chunks):
        gate = _dot(h, win_ref[:, c * cw:(c + 1) * cw])
        up = _dot(h, win_ref[:, d_ff + c * cw:d_ff + (c + 1) * cw])
        a = (gate * _sigmoid(gate) * up).astype(BF16)
        part = _dot(a, wout_ref[c * cw:(c + 1) * cw, :])
        acc = part if acc is None else acc + part
    _store_rows(i, np_t, x + acc, op_ref, os_ref)


def _ffn_call(xp, xs, g, w_in, w_out, *, alias):
    d = xp.shape[1]
    d_ff = w_out.shape[0]
    np_t, ns_t = xp.shape[0] // ROW_TILE, xs.shape[0] // ROW_TILE
    n_chunks = 2 if d_ff % (2 * LANES) == 0 else 1
    p_spec, s_spec = _split_specs(np_t, d)
    return pl.pallas_call(
        functools.partial(_ffn_kernel, np_t=np_t, d_ff=d_ff, n_chunks=n_chunks),
        grid=(np_t + ns_t,),
        in_specs=[p_spec, s_spec, _resident((1, d)), _resident(w_in.shape), _resident(w_out.shape)],
        out_specs=[p_spec, s_spec],
        out_shape=[jax.ShapeDtypeStruct(xp.shape, F32), jax.ShapeDtypeStruct(xs.shape, F32)],
        input_output_aliases={0: 0, 1: 1} if alias else {},
        compiler_params=_params(("arbitrary",)),
        name="ffn",
    )(xp, xs, g, w_in, w_out)


def _qkv_kernel(xp_ref, xs_ref, g_ref, w_ref, gq_ref, gk_ref, gmat_ref,
                q_ref, k_ref, v_ref, kt_ref, vt_ref, ks_ref, vs_ref, *, np_t, tiles_per_batch):
    i = pl.program_id(0)
    d = xp_ref.shape[1]
    h = _rms(_load_rows(i, np_t, xp_ref, xs_ref), g_ref[...]).astype(BF16)
    qkv = _dot(h, w_ref[...])
    gmat = gmat_ref[...]

    def head_norm(t, g):
        sq = (t * t).astype(BF16)
        ms = jnp.concatenate([_dot(sq[:, j * MXU_DIM:(j + 1) * MXU_DIM], gmat)
                              for j in range(d // MXU_DIM)], axis=1)
        return t * lax.rsqrt(ms + EPS) * g

    qn = head_norm(qkv[:, :d], gq_ref[...])
    kn = head_norm(qkv[:, d:2 * d], gk_ref[...])
    v = qkv[:, 2 * d:]
    for p in range(d // LANES):
        sl = slice(p * LANES, (p + 1) * LANES)
        q_ref[p] = qn[:, sl].astype(BF16)
        k_ref[p] = kn[:, sl].astype(BF16)
        v_ref[p] = v[:, sl].astype(BF16)

    @pl.when(jnp.logical_and(i < np_t, i % tiles_per_batch == tiles_per_batch - 1))
    def _():
        kt_ref[...] = kn.T
        vt_ref[...] = v.T

    @pl.when(i >= np_t)
    def _():
        ks_ref[...] = kn
        vs_ref[...] = v


def _qkv_call(xp, xs, g, w, gq, gk, gmat, *, n_batch, tiles_per_batch):
    d = xp.shape[1]
    np_t, ns_t = xp.shape[0] // ROW_TILE, xs.shape[0] // ROW_TILE
    n_rows = xp.shape[0] + xs.shape[0]
    n_pairs = d // LANES
    p_spec, s_spec = _split_specs(np_t, d)
    pm_spec = pl.BlockSpec((n_pairs, ROW_TILE, LANES), lambda i: (0, i, 0))
    pm_shape = jax.ShapeDtypeStruct((n_pairs, n_rows, LANES), BF16)
    t_spec = pl.BlockSpec((None, d, ROW_TILE), lambda i: (jnp.minimum(i // tiles_per_batch, n_batch - 1), 0, 0))
    t_shape = jax.ShapeDtypeStruct((n_batch, d, ROW_TILE), F32)
    return pl.pallas_call(
        functools.partial(_qkv_kernel, np_t=np_t, tiles_per_batch=tiles_per_batch),
        grid=(np_t + ns_t,),
        in_specs=[p_spec, s_spec, _resident((1, d)), _resident(w.shape), _resident((1, d)), _resident((1, d)),
                  _resident(gmat.shape)],
        out_specs=[pm_spec, pm_spec, pm_spec, t_spec, t_spec, s_spec, s_spec],
        out_shape=[pm_shape, pm_shape, pm_shape, t_shape, t_shape,
                   jax.ShapeDtypeStruct(xs.shape, F32), jax.ShapeDtypeStruct(xs.shape, F32)],
        compiler_params=_params(("arbitrary",)),
        name="qkv",
    )(xp, xs, g, w, gq, gk, gmat)


def _bias_kernel(r_ref, o_ref):
    n_keys = o_ref.shape[2]
    width = r_ref.shape[2]
    x = jnp.broadcast_to(r_ref[0], (Q_TILE, width))
    row = lax.broadcasted_iota(jnp.int32, (Q_TILE, width), 0)
    shift = 1
    while shift < Q_TILE:
        x = jnp.where((row & shift) != 0, pltpu.roll(x, shift, 1), x)
        shift *= 2
    x = x[:, :n_keys]
    qc = lax.broadcasted_iota(jnp.int32, (Q_TILE, n_keys), 0) // CHUNK
    kc = lax.broadcasted_iota(jnp.int32, (Q_TILE, n_keys), 1) // CHUNK
    in_band = jnp.logical_and(kc >= qc, kc <= qc + PAST_CHUNKS)
    o_ref[0] = jnp.where(in_band, x, NEG_INF)


def _bias_call(rel_table):
    n_heads, n_rel = rel_table.shape
    clip = (n_rel - 1) // 2
    n_keys = Q_TILE + BAND_PAST
    width = 1024
    assert width >= n_keys + Q_TILE and BAND_PAST >= clip
    far = jnp.broadcast_to(rel_table[:, n_rel - 1:], (n_heads, BAND_PAST - clip))
    near = jnp.broadcast_to(rel_table[:, :1], (n_heads, n_keys - (BAND_PAST + clip) - 1))
    wrap = jnp.broadcast_to(rel_table[:, n_rel - 1:], (n_heads, width - n_keys))
    gen = jnp.concatenate([far, rel_table[:, ::-1], near, wrap], axis=1).reshape(n_heads, 1, width)
    return pl.pallas_call(
        _bias_kernel,
        grid=(n_heads,),
        in_specs=[pl.BlockSpec((1, 1, width), lambda h: (h, 0, 0))],
        out_specs=pl.BlockSpec((1, Q_TILE, n_keys), lambda h: (h, 0, 0)),
        out_shape=jax.ShapeDtypeStruct((n_heads, Q_TILE, n_keys), F32),
        compiler_params=_params(("arbitrary",)),
        name="rel_bias",
    )(gen)


def _split_heads(qp, lo):
    zero = jnp.zeros_like(qp)
    return jnp.concatenate([jnp.where(lo, qp, zero), jnp.where(lo, zero, qp)], axis=0)


def _attn_kernel(q_ref, k0_ref, k1_ref, k2_ref, v0_ref, v1_ref, v2_ref, bias_ref, x_ref, wo_ref,
                 o_ref, opair_ref):
    t = pl.program_id(1)
    n_pairs = q_ref.shape[0]
    pen0 = jnp.where(t < 2, NEG_INF, 0.0).astype(F32)
    pen1 = jnp.where(t < 1, NEG_INF, 0.0).astype(F32)
    lo = lax.broadcasted_iota(jnp.int32, (Q_TILE, LANES), 1) < LANES // 2

    def pair_body(p, carry):
        qs = _split_heads(q_ref[p], lo)
        s0 = _dot_nt(qs, k0_ref[p]) + bias_ref[p, :, 0:Q_TILE]
        s1 = _dot_nt(qs, k1_ref[p]) + bias_ref[p, :, Q_TILE:2 * Q_TILE]
        s2 = _dot_nt(qs, k2_ref[p]) + bias_ref[p, :, 2 * Q_TILE:3 * Q_TILE]
        m = jnp.max(jnp.maximum(jnp.maximum(_fold(jnp.maximum, s0) + pen0, _fold(jnp.maximum, s1) + pen1),
                                _fold(jnp.maximum, s2)), axis=1, keepdims=True)
        e0 = jnp.exp(s0 - (m - pen0))
        e1 = jnp.exp(s1 - (m - pen1))
        e2 = jnp.exp(s2 - m)
        denom = jnp.sum(_fold(jnp.add, e0) + _fold(jnp.add, e1) + _fold(jnp.add, e2), axis=1, keepdims=True)
        o = (_dot(e0.astype(BF16), v0_ref[p]) + _dot(e1.astype(BF16), v1_ref[p])
             + _dot(e2.astype(BF16), v2_ref[p]))
        o = o * (1.0 / denom)
        opair_ref[p] = jnp.where(lo, o[:Q_TILE], o[Q_TILE:]).astype(BF16)
        return carry

    lax.fori_loop(0, n_pairs, pair_body, 0, unroll=ATTN_PAIR_UNROLL)
    o_all = jnp.concatenate([opair_ref[p] for p in range(n_pairs)], axis=1)
    o_ref[...] = x_ref[...] + _dot(o_all, wo_ref[...])


def _attn_call(xp, q, k, v, bias2, w_o, *, n_batch, seq, alias):
    d = xp.shape[1]
    n_pairs = q.shape[0]
    nt = seq // Q_TILE

    def blk(back):
        return pl.BlockSpec((n_pairs, Q_TILE, LANES),
                            lambda b, t: (0, b * nt + jnp.maximum(t - back, 0), 0))

    row_spec = pl.BlockSpec((Q_TILE, d), lambda b, t: (b * nt + t, 0))
    return pl.pallas_call(
        _attn_kernel,
        grid=(n_batch, nt),
        in_specs=[blk(0), blk(2), blk(1), blk(0), blk(2), blk(1), blk(0),
                  _resident(bias2.shape), row_spec, _resident(w_o.shape)],
        out_specs=row_spec,
        out_shape=jax.ShapeDtypeStruct(xp.shape, F32),
        scratch_shapes=[pltpu.VMEM((n_pairs, Q_TILE, LANES), BF16)],
        input_output_aliases={8: 0} if alias else {},
        compiler_params=_params(("arbitrary", "arbitrary")),
        name="attn_prompt",
    )(q, k, k, k, v, v, v, bias2, xp, w_o)


def _attn_sample_kernel(q_ref, k_ref, v_ref, ckt_ref, cvt_ref, bias_ref, x_ref, wo_ref, o_ref):
    n_pairs, n_new, _ = q_ref.shape
    n_past = ckt_ref.shape[1]
    lo = lax.broadcasted_iota(jnp.int32, (n_new, LANES), 1) < LANES // 2
    outs = []
    for p in range(n_pairs):
        sl = slice(p * LANES, (p + 1) * LANES)
        qs = _split_heads(q_ref[p], lo)
        bias = jnp.concatenate([bias_ref[p, 0:n_new, :], bias_ref[p, Q_TILE:Q_TILE + n_new, :]], axis=0)
        sc = _dot(qs, ckt_ref[sl, :].astype(BF16)) + bias[:, 0:n_past]
        sn = _dot_nt(qs, k_ref[p]) + bias[:, n_past:n_past + n_new]
        m = jnp.maximum(jnp.max(sc, axis=1, keepdims=True), jnp.max(sn, axis=1, keepdims=True))
        ec = jnp.exp(sc - m)
        en = jnp.exp(sn - m)
        denom = jnp.sum(ec, axis=1, keepdims=True) + jnp.sum(en, axis=1, keepdims=True)
        o = _dot_nt(ec.astype(BF16), cvt_ref[sl, :].astype(BF16)) + _dot(en.astype(BF16), v_ref[p])
        o = o * (1.0 / denom)
        outs.append(jnp.where(lo, o[:n_new], o[n_new:]).astype(BF16))
    o_all = jnp.concatenate(outs, axis=1)
    o_ref[...] = x_ref[...] + _dot(o_all, wo_ref[...])


def _attn_sample_call(xs, q, k, v, cache_kt, cache_vt, layer, bias2, w_o, *, n_dec, n_new, row0, alias):
    d = xs.shape[1]
    n_pairs = q.shape[0]
    n_past = cache_kt.shape[3]
    blk0 = row0 // n_new
    pm_spec = pl.BlockSpec((n_pairs, n_new, LANES), lambda b: (0, blk0 + b, 0))
    cache_spec = pl.BlockSpec((None, None, d, n_past), lambda b: (layer, b, 0, 0))
    row_spec = pl.BlockSpec((n_new, d), lambda b: (b, 0))
    return pl.pallas_call(
        _attn_sample_kernel,
        grid=(n_dec,),
        in_specs=[pm_spec, pm_spec, pm_spec, cache_spec, cache_spec,
                  _resident(bias2.shape), row_spec, _resident(w_o.shape)],
        out_specs=row_spec,
        out_shape=jax.ShapeDtypeStruct(xs.shape, F32),
        input_output_aliases={6: 0} if alias else {},
        compiler_params=_params(("arbitrary",)),
        name="attn_sample",
    )(q, k, v, cache_kt, cache_vt, bias2, xs, w_o)


def _glu_kernel(xp_ref, xs_ref, g_ref, w_ref, b_ref, u_ref, last_ref, *, np_t, tiles_per_batch):
    i = pl.program_id(0)
    d = xp_ref.shape[1]
    h = _rms(_load_rows(i, np_t, xp_ref, xs_ref), g_ref[...]).astype(BF16)
    a = _dot(h, w_ref[...]) + b_ref[...]
    u = a[:, :d] * _sigmoid(a[:, d:])
    u_ref[...] = u

    @pl.when(jnp.logical_and(i < np_t, i % tiles_per_batch == tiles_per_batch - 1))
    def _():
        last_ref[...] = u[ROW_TILE - HALO_ROWS:, :]


def _glu_call(xp, xs, g, w, b, *, n_batch, tiles_per_batch):
    d = xp.shape[1]
    np_t, ns_t = xp.shape[0] // ROW_TILE, xs.shape[0] // ROW_TILE
    n_rows = xp.shape[0] + xs.shape[0]
    p_spec, s_spec = _split_specs(np_t, d)
    last_spec = pl.BlockSpec((HALO_ROWS, d), lambda i: (jnp.minimum(i // tiles_per_batch, n_batch - 1), 0))
    return pl.pallas_call(
        functools.partial(_glu_kernel, np_t=np_t, tiles_per_batch=tiles_per_batch),
        grid=(np_t + ns_t,),
        in_specs=[p_spec, s_spec, _resident((1, d)), _resident(w.shape), _resident((1, 2 * d))],
        out_specs=[pl.BlockSpec((ROW_TILE, d), lambda i: (i, 0)), last_spec],
        out_shape=[jax.ShapeDtypeStruct((n_rows, d), F32), jax.ShapeDtypeStruct((n_batch * HALO_ROWS, d), F32)],
        compiler_params=_params(("arbitrary",)),
        name="conv_glu",
    )(xp, xs, g, w, b)


def _conv_tail(ctx_ref, y_ref, x, dw_ref, dwb_ref, lng_ref, lnb_ref, w2_ref, b2_ref):
    n_slabs, n_rows, _ = y_ref.shape
    conv_w = dw_ref.shape[0]
    first = HALO_ROWS - (conv_w - 1)
    rb = min(CONV_ROW_BLOCK, n_rows)
    per_phase = rb // CONV_ROW_STRIDE

    def row_block(i, carry):
        r0 = pl.multiple_of(i * rb, rb)
        for c in range(n_slabs):
            sl = slice(c * LANES, (c + 1) * LANES)
            win = ctx_ref.at[c, pl.ds(r0, rb + HALO_ROWS), :]
            out = y_ref.at[c, pl.ds(r0, rb), :]
            for ph in range(CONV_ROW_STRIDE):
                acc = jnp.zeros((per_phase, LANES), F32)
                for k in range(conv_w):
                    acc = acc + dw_ref[k:k + 1, sl] * win[pl.ds(first + k + ph, per_phase, stride=CONV_ROW_STRIDE), :]
                out[pl.ds(ph, per_phase, stride=CONV_ROW_STRIDE), :] = acc + dwb_ref[:, sl]
        return carry

    lax.fori_loop(0, n_rows // rb, row_block, 0)
    y = jnp.concatenate([y_ref[c] for c in range(n_slabs)], axis=1)
    yc = y - jnp.mean(y, axis=-1, keepdims=True)
    yn = yc * lax.rsqrt(jnp.mean(yc * yc, axis=-1, keepdims=True) + EPS) * lng_ref[...] + lnb_ref[...]
    act = (yn * _sigmoid(yn)).astype(BF16)
    return x + _dot(act, w2_ref[...]) + b2_ref[...]


def _fill_ctx(ctx_ref, halo, u_ref):
    for c in range(ctx_ref.shape[0]):
        sl = slice(c * LANES, (c + 1) * LANES)
        ctx_ref[c, 0:HALO_ROWS, :] = halo[:, sl]
        ctx_ref[c, HALO_ROWS:, :] = u_ref[:, sl]


def _conv_scratch(n_rows, d):
    return [pltpu.VMEM((d // LANES, HALO_ROWS + n_rows, LANES), F32), pltpu.VMEM((d // LANES, n_rows, LANES), F32)]


def _conv_kernel(halo_ref, u_ref, x_ref, dw_ref, dwb_ref, lng_ref, lnb_ref, w2_ref, b2_ref, o_ref,
                 ctx_ref, y_ref):
    t = pl.program_id(1)
    halo = halo_ref[...]
    _fill_ctx(ctx_ref, jnp.where(t > 0, halo, jnp.zeros_like(halo)), u_ref)
    o_ref[...] = _conv_tail(ctx_ref, y_ref, x_ref[...], dw_ref, dwb_ref, lng_ref, lnb_ref, w2_ref, b2_ref)


def _conv_call(xp, u, dw, dwb, lng, lnb, w2, b2, *, n_batch, seq):
    d = xp.shape[1]
    nt = seq // ROW_TILE
    per_halo = ROW_TILE // HALO_ROWS
    row_spec = pl.BlockSpec((ROW_TILE, d), lambda b, t: (b * nt + t, 0))
    halo_spec = pl.BlockSpec((HALO_ROWS, d), lambda b, t: (jnp.maximum((b * nt + t) * per_halo - 1, 0), 0))
    return pl.pallas_call(
        _conv_kernel,
        grid=(n_batch, nt),
        in_specs=[halo_spec, row_spec, row_spec, _resident(dw.shape), _resident((1, d)), _resident((1, d)),
                  _resident((1, d)), _resident(w2.shape), _resident((1, d))],
        out_specs=row_spec,
        out_shape=jax.ShapeDtypeStruct(xp.shape, F32),
        scratch_shapes=_conv_scratch(ROW_TILE, d),
        input_output_aliases={2: 0},
        compiler_params=_params(("arbitrary", "arbitrary")),
        name="conv_prompt",
    )(u, u, xp, dw, dwb, lng, lnb, w2, b2)


def _conv_sample_kernel(state_ref, u_ref, x_ref, dw_ref, dwb_ref, lng_ref, lnb_ref, w2_ref, b2_ref, o_ref,
                        ctx_ref, y_ref):
    _fill_ctx(ctx_ref, state_ref[...], u_ref)
    o_ref[...] = _conv_tail(ctx_ref, y_ref, x_ref[...], dw_ref, dwb_ref, lng_ref, lnb_ref, w2_ref, b2_ref)


def _conv_sample_call(xs, u, state, layer, dw, dwb, lng, lnb, w2, b2, *, n_dec, n_new, row0):
    d = xs.shape[1]
    blk0 = row0 // n_new
    row_spec = pl.BlockSpec((n_new, d), lambda b: (b, 0))
    u_spec = pl.BlockSpec((n_new, d), lambda b: (blk0 + b, 0))
    state_spec = pl.BlockSpec((None, None, HALO_ROWS, d), lambda b: (layer, b, 0, 0))
    return pl.pallas_call(
        _conv_sample_kernel,
        grid=(n_dec,),
        in_specs=[state_spec, u_spec, row_spec, _resident(dw.shape), _resident((1, d)), _resident((1, d)),
                  _resident((1, d)), _resident(w2.shape), _resident((1, d))],
        out_specs=row_spec,
        out_shape=jax.ShapeDtypeStruct(xs.shape, F32),
        scratch_shapes=_conv_scratch(n_new, d),
        input_output_aliases={2: 0},
        compiler_params=_params(("arbitrary",)),
        name="conv_sample",
    )(state, u, xs, dw, dwb, lng, lnb, w2, b2)


def kernel(x_prompt, x_sample, cache_k, cache_v, state_conv, norm_mix, norm_ffn, w_qkv, q_norm, k_norm,
           rel_table, w_o, pw1_w, pw1_b, dw_w, dw_b, conv_ln_g, conv_ln_b, pw2_w, pw2_b, ffn_w_in, ffn_w_out):
    n_batch, seq, d = x_prompt.shape
    n_dec, n_new, _ = x_sample.shape
    depth = norm_mix.shape[0]
    head_dim = q_norm.shape[1]
    n_heads = d // head_dim
    conv_w = dw_w.shape[1]
    n_prompt = n_batch * seq
    n_sample = n_dec * n_new
    keep = min(BAND_PAST, seq)
    tiles_per_batch = seq // ROW_TILE
    assert seq % ROW_TILE == 0 and n_sample % ROW_TILE == 0 and keep == ROW_TILE
    assert LANES == 2 * head_dim and d % MXU_DIM == 0 and n_new % 8 == 0
    assert conv_w - 1 <= HALO_ROWS and n_new >= conv_w - 1
    assert cache_k.shape[2] == BAND_PAST and cache_k.shape[1] == n_dec

    xp = x_prompt.reshape(n_prompt, d)
    xs = x_sample.reshape(n_sample, d)

    blk = jnp.arange(MXU_DIM) // head_dim
    gmat = jnp.where(blk[:, None] == blk[None, :], 1.0 / head_dim, 0.0).astype(BF16)
    inv_sqrt_dh = float(head_dim) ** -0.5
    cache_kt = jnp.transpose(cache_k, (0, 1, 3, 4, 2)).reshape(cache_k.shape[0], n_dec, d, BAND_PAST)
    cache_vt = jnp.transpose(cache_v, (0, 1, 3, 4, 2)).reshape(cache_v.shape[0], n_dec, d, BAND_PAST)
    state_pad = jnp.pad(state_conv, ((0, 0), (0, 0), (HALO_ROWS - (conv_w - 1), 0), (0, 0)))

    def heads_last(t):
        return jnp.transpose(t.reshape(t.shape[0], n_heads, head_dim, t.shape[2]), (0, 3, 1, 2))

    kp_new, vp_new, ks_new, vs_new, cp_new, cs_new = [], [], [], [], [], []
    for layer in range(depth):
        g_mix = norm_mix[layer].reshape(1, d)
        own = layer > 0
        if layer % 2 == 0:
            a = layer // 2
            gq = (jnp.tile(q_norm[a], n_heads) * inv_sqrt_dh).reshape(1, d)
            gk = jnp.tile(k_norm[a], n_heads).reshape(1, d)
            q, k, v, kt, vt, ksf, vsf = _qkv_call(xp, xs, g_mix, w_qkv[a].astype(BF16), gq, gk, gmat,
                                                  n_batch=n_batch, tiles_per_batch=tiles_per_batch)
            bias = _bias_call(rel_table[a])
            bias2 = bias.reshape(n_heads // 2, 2 * Q_TILE, Q_TILE + BAND_PAST)
            wo = w_o[a].astype(BF16)
            xp = _attn_call(xp, q, k, v, bias2, wo, n_batch=n_batch, seq=seq, alias=own)
            xs = _attn_sample_call(xs, q, k, v, cache_kt, cache_vt, a, bias2, wo,
                                   n_dec=n_dec, n_new=n_new, row0=n_prompt, alias=own)
            kp_new.append(heads_last(kt))
            vp_new.append(heads_last(vt))
            ks_new.append(ksf.reshape(n_dec, n_new, n_heads, head_dim))
            vs_new.append(vsf.reshape(n_dec, n_new, n_heads, head_dim))
        else:
            c = layer // 2
            u, u_last = _glu_call(xp, xs, g_mix, pw1_w[c].astype(BF16), pw1_b[c].reshape(1, 2 * d),
                                  n_batch=n_batch, tiles_per_batch=tiles_per_batch)
            conv_args = (dw_w[c], dw_b[c].reshape(1, d), conv_ln_g[c].reshape(1, d), conv_ln_b[c].reshape(1, d),
                         pw2_w[c].astype(BF16), pw2_b[c].reshape(1, d))
            xp = _conv_call(xp, u, *conv_args, n_batch=n_batch, seq=seq)
            xs = _conv_sample_call(xs, u, state_pad, c, *conv_args, n_dec=n_dec, n_new=n_new, row0=n_prompt)
            cp_new.append(u_last.reshape(n_batch, HALO_ROWS, d)[:, HALO_ROWS - (conv_w - 1):])
            cs_new.append(u[n_prompt:].reshape(n_dec, n_new, d)[:, n_new - (conv_w - 1):])
        xp, xs = _ffn_call(xp, xs, norm_ffn[layer].reshape(1, d), ffn_w_in[layer].astype(BF16),
                           ffn_w_out[layer].astype(BF16), alias=True)

    return (xp.reshape(n_batch, seq, d), xs.reshape(n_dec, n_new, d),
            jnp.stack(kp_new), jnp.stack(vp_new), jnp.stack(ks_new), jnp.stack(vs_new),
            jnp.stack(cp_new), jnp.stack(cs_new))
```

```python
import functools

import jax
import jax.numpy as jnp
from jax import lax
from jax.experimental import pallas as pl
from jax.experimental.pallas import tpu as pltpu

CHUNK = 64
PAST_CHUNKS = 8
BAND_PAST = PAST_CHUNKS * CHUNK
EPS = 1e-6
NEG_INF = -1e30

LANES = 128
MXU_DIM = 256
VMEM_LIMIT_BYTES = 58 * 1024 * 1024

ROW_TILE = 512
Q_TILE = 256
ATTN_TILE = 512
CONV_ROW_BLOCK = 64
CONV_ROW_STRIDE = 2
CONV_STEP_SLABS = 8
HALO_ROWS = 32

BF16 = jnp.bfloat16
F32 = jnp.float32


def _dot(a, b):
    return jnp.dot(a, b, preferred_element_type=F32)


def _dot_nt(a, b):
    return lax.dot_general(a, b, (((1,), (1,)), ((), ())), preferred_element_type=F32)


def _rms(x, g):
    ms = jnp.mean(x * x, axis=-1, keepdims=True)
    return x * lax.rsqrt(ms + EPS) * g


def _sigmoid(x):
    return 1.0 / (1.0 + jnp.exp(-x))


def _fold(op, x):
    acc = x[:, 0:LANES]
    for j in range(1, x.shape[1] // LANES):
        acc = op(acc, x[:, j * LANES:(j + 1) * LANES])
    return acc


def _resident(shape):
    nd = len(shape)
    return pl.BlockSpec(shape, lambda *_: (0,) * nd, pipeline_mode=pl.Buffered(1))


def _params(sem):
    return pltpu.CompilerParams(dimension_semantics=sem, vmem_limit_bytes=VMEM_LIMIT_BYTES)


def _split_specs(np_t, d):
    return (pl.BlockSpec((ROW_TILE, d), lambda i: (jnp.minimum(i, np_t - 1), 0)),
            pl.BlockSpec((ROW_TILE, d), lambda i: (jnp.maximum(i - np_t, 0), 0)))


def _load_rows(i, np_t, xp_ref, xs_ref):
    return jnp.where(i < np_t, xp_ref[...], xs_ref[...])


def _store_rows(i, np_t, y, op_ref, os_ref):
    @pl.when(i < np_t)
    def _():
        op_ref[...] = y

    @pl.when(i >= np_t)
    def _():
        os_ref[...] = y


def _swiglu(x, g, win_ref, wout_ref):
    d_ff = wout_ref.shape[0]
    n_tiles = d_ff // MXU_DIM
    bounds = (0, (n_tiles + 1) // 2 * MXU_DIM, d_ff)
    h = _rms(x, g).astype(BF16)
    acc = x
    for lo, hi in zip(bounds[:-1], bounds[1:]):
        gate = _dot(h, win_ref[:, lo:hi])
        up = _dot(h, win_ref[:, d_ff + lo:d_ff + hi])
        a = (gate * _sigmoid(gate) * up).astype(BF16)
        acc = acc + _dot(a, wout_ref[lo:hi, :])
    return acc


def _ffn_kernel(xp_ref, xs_ref, g_ref, win_ref, wout_ref, op_ref, os_ref, *, np_t):
    i = pl.program_id(0)
    x = _load_rows(i, np_t, xp_ref, xs_ref)
    _store_rows(i, np_t, _swiglu(x, g_ref[...], win_ref, wout_ref), op_ref, os_ref)


def _ffn_call(xp, xs, g, w_in, w_out, *, alias):
    d = xp.shape[1]
    assert w_out.shape[0] % MXU_DIM == 0
    np_t, ns_t = xp.shape[0] // ROW_TILE, xs.shape[0] // ROW_TILE
    p_spec, s_spec = _split_specs(np_t, d)
    return pl.pallas_call(
        functools.partial(_ffn_kernel, np_t=np_t),
        grid=(np_t + ns_t,),
        in_specs=[p_spec, s_spec, _resident((1, d)), _resident(w_in.shape), _resident(w_out.shape)],
        out_specs=[p_spec, s_spec],
        out_shape=[jax.ShapeDtypeStruct(xp.shape, F32), jax.ShapeDtypeStruct(xs.shape, F32)],
        input_output_aliases={0: 0, 1: 1} if alias else {},
        compiler_params=_params(("arbitrary",)),
        name="ffn",
    )(xp, xs, g, w_in, w_out)


def _qkv_kernel(xp_ref, xs_ref, g_ref, w_ref, gq_ref, gk_ref, gmat_ref,
                q_ref, k_ref, v_ref, kt_ref, vt_ref, ks_ref, vs_ref, *, np_t, tiles_per_batch):
    i = pl.program_id(0)
    d = xp_ref.shape[1]
    h = _rms(_load_rows(i, np_t, xp_ref, xs_ref), g_ref[...]).astype(BF16)
    qkv = _dot(h, w_ref[...])
    gmat = gmat_ref[...]

    def head_norm(t, g):
        sq = (t * t).astype(BF16)
        ms = jnp.concatenate([_dot(sq[:, j * MXU_DIM:(j + 1) * MXU_DIM], gmat)
                              for j in range(d // MXU_DIM)], axis=1)
        return t * lax.rsqrt(ms + EPS) * g

    qn = head_norm(qkv[:, :d], gq_ref[...])
    kn = head_norm(qkv[:, d:2 * d], gk_ref[...])
    v = qkv[:, 2 * d:]
    for p in range(d // LANES):
        sl = slice(p * LANES, (p + 1) * LANES)
        q_ref[p] = qn[:, sl].astype(BF16)
        k_ref[p] = kn[:, sl].astype(BF16)
        v_ref[p] = v[:, sl].astype(BF16)

    @pl.when(jnp.logical_and(i < np_t, i % tiles_per_batch == tiles_per_batch - 1))
    def _():
        kt_ref[...] = kn.T
        vt_ref[...] = v.T

    @pl.when(i >= np_t)
    def _():
        ks_ref[...] = kn
        vs_ref[...] = v


def _qkv_call(xp, xs, g, w, gq, gk, gmat, *, n_batch, tiles_per_batch):
    d = xp.shape[1]
    np_t, ns_t = xp.shape[0] // ROW_TILE, xs.shape[0] // ROW_TILE
    n_rows = xp.shape[0] + xs.shape[0]
    n_pairs = d // LANES
    p_spec, s_spec = _split_specs(np_t, d)
    pm_spec = pl.BlockSpec((n_pairs, ROW_TILE, LANES), lambda i: (0, i, 0))
    pm_shape = jax.ShapeDtypeStruct((n_pairs, n_rows, LANES), BF16)
    t_spec = pl.BlockSpec((None, d, ROW_TILE), lambda i: (jnp.minimum(i // tiles_per_batch, n_batch - 1), 0, 0))
    t_shape = jax.ShapeDtypeStruct((n_batch, d, ROW_TILE), F32)
    return pl.pallas_call(
        functools.partial(_qkv_kernel, np_t=np_t, tiles_per_batch=tiles_per_batch),
        grid=(np_t + ns_t,),
        in_specs=[p_spec, s_spec, _resident((1, d)), _resident(w.shape), _resident((1, d)), _resident((1, d)),
                  _resident(gmat.shape)],
        out_specs=[pm_spec, pm_spec, pm_spec, t_spec, t_spec, s_spec, s_spec],
        out_shape=[pm_shape, pm_shape, pm_shape, t_shape, t_shape,
                   jax.ShapeDtypeStruct(xs.shape, F32), jax.ShapeDtypeStruct(xs.shape, F32)],
        compiler_params=_params(("arbitrary",)),
        name="qkv",
    )(xp, xs, g, w, gq, gk, gmat)


def _bias_kernel(r_ref, o_ref):
    n_keys = o_ref.shape[2]
    width = r_ref.shape[2]
    x = jnp.broadcast_to(r_ref[0], (Q_TILE, width))
    row = lax.broadcasted_iota(jnp.int32, (Q_TILE, width), 0)
    shift = 1
    while shift < Q_TILE:
        x = jnp.where((row & shift) != 0, pltpu.roll(x, shift, 1), x)
        shift *= 2
    x = x[:, :n_keys]
    qc = lax.broadcasted_iota(jnp.int32, (Q_TILE, n_keys), 0) // CHUNK
    kc = lax.broadcasted_iota(jnp.int32, (Q_TILE, n_keys), 1) // CHUNK
    in_band = jnp.logical_and(kc >= qc, kc <= qc + PAST_CHUNKS)
    o_ref[0] = jnp.where(in_band, x, NEG_INF)


def _bias_call(rel_table):
    n_heads, n_rel = rel_table.shape
    clip = (n_rel - 1) // 2
    n_keys = Q_TILE + BAND_PAST
    width = 1024
    assert width >= n_keys + Q_TILE and BAND_PAST >= clip
    far = jnp.broadcast_to(rel_table[:, n_rel - 1:], (n_heads, BAND_PAST - clip))
    near = jnp.broadcast_to(rel_table[:, :1], (n_heads, n_keys - (BAND_PAST + clip) - 1))
    wrap = jnp.broadcast_to(rel_table[:, n_rel - 1:], (n_heads, width - n_keys))
    gen = jnp.concatenate([far, rel_table[:, ::-1], near, wrap], axis=1).reshape(n_heads, 1, width)
    return pl.pallas_call(
        _bias_kernel,
        grid=(n_heads,),
        in_specs=[pl.BlockSpec((1, 1, width), lambda h: (h, 0, 0))],
        out_specs=pl.BlockSpec((1, Q_TILE, n_keys), lambda h: (h, 0, 0)),
        out_shape=jax.ShapeDtypeStruct((n_heads, Q_TILE, n_keys), F32),
        compiler_params=_params(("arbitrary",)),
        name="rel_bias",
    )(gen)


def _split_heads(qp, lo):
    zero = jnp.zeros_like(qp)
    return jnp.concatenate([jnp.where(lo, qp, zero), jnp.where(lo, zero, qp)], axis=0)


def _attn_kernel(q_ref, kp_ref, kc_ref, vp_ref, vc_ref, bias_ref, x_ref, wo_ref, o_ref, opair_ref):
    t = pl.program_id(1)
    n_pairs = q_ref.shape[0]
    blocks_per_tile = ATTN_TILE // Q_TILE
    pen_prev = jnp.where(t < 1, NEG_INF, 0.0).astype(F32)
    lo = lax.broadcasted_iota(jnp.int32, (Q_TILE, LANES), 1) < LANES // 2

    for g in range(blocks_per_tile):
        rows = slice(g * Q_TILE, (g + 1) * Q_TILE)
        for p in range(n_pairs):
            qs = _split_heads(q_ref[p, rows, :], lo)
            s, pens, vals = [], [], []
            for j in range(BAND_PAST // Q_TILE + 1):
                blk = g + j
                from_prev = blk < blocks_per_tile
                k_ref, v_ref = (kp_ref, vp_ref) if from_prev else (kc_ref, vc_ref)
                krows = slice((blk % blocks_per_tile) * Q_TILE, (blk % blocks_per_tile + 1) * Q_TILE)
                s.append(_dot_nt(qs, k_ref[p, krows, :]) + bias_ref[p, :, j * Q_TILE:(j + 1) * Q_TILE])
                pens.append(pen_prev if from_prev else None)
                vals.append(v_ref[p, krows, :])
            mx = None
            for sj, pen in zip(s, pens):
                fj = _fold(jnp.maximum, sj)
                fj = fj if pen is None else fj + pen
                mx = fj if mx is None else jnp.maximum(mx, fj)
            m = jnp.max(mx, axis=1, keepdims=True)
            e = [jnp.exp(sj - (m if pen is None else m - pen)) for sj, pen in zip(s, pens)]
            denom = jnp.sum(sum(_fold(jnp.add, ej) for ej in e), axis=1, keepdims=True)
            o = sum(_dot(ej.astype(BF16), vj) for ej, vj in zip(e, vals))
            o = o * (1.0 / denom)
            opair_ref[p, rows, :] = jnp.where(lo, o[:Q_TILE], o[Q_TILE:]).astype(BF16)
    o_all = jnp.concatenate([opair_ref[p] for p in range(n_pairs)], axis=1)
    o_ref[...] = x_ref[...] + _dot(o_all, wo_ref[...])


def _attn_call(xp, q, k, v, bias2, w_o, *, n_batch, seq, alias):
    d = xp.shape[1]
    n_pairs = q.shape[0]
    nt = seq // ATTN_TILE
    assert ATTN_TILE == BAND_PAST and ATTN_TILE % Q_TILE == 0 and seq % ATTN_TILE == 0

    def blk(back):
        return pl.BlockSpec((n_pairs, ATTN_TILE, LANES),
                            lambda b, t: (0, b * nt + jnp.maximum(t - back, 0), 0))

    row_spec = pl.BlockSpec((ATTN_TILE, d), lambda b, t: (b * nt + t, 0))
    return pl.pallas_call(
        _attn_kernel,
        grid=(n_batch, nt),
        in_specs=[blk(0), blk(1), blk(0), blk(1), blk(0),
                  _resident(bias2.shape), row_spec, _resident(w_o.shape)],
        out_specs=row_spec,
        out_shape=jax.ShapeDtypeStruct(xp.shape, F32),
        scratch_shapes=[pltpu.VMEM((n_pairs, ATTN_TILE, LANES), BF16)],
        input_output_aliases={6: 0} if alias else {},
        compiler_params=_params(("arbitrary", "arbitrary")),
        name="attn_prompt",
    )(q, k, k, v, v, bias2, xp, w_o)


def _attn_sample_kernel(q_ref, k_ref, v_ref, ckt_ref, cvt_ref, bias_ref, x_ref, wo_ref, o_ref):
    n_pairs, n_new, _ = q_ref.shape
    n_past = ckt_ref.shape[1]
    lo = lax.broadcasted_iota(jnp.int32, (n_new, LANES), 1) < LANES // 2
    outs = []
    for p in range(n_pairs):
        sl = slice(p * LANES, (p + 1) * LANES)
        qs = _split_heads(q_ref[p], lo)
        bias = jnp.concatenate([bias_ref[p, 0:n_new, :], bias_ref[p, Q_TILE:Q_TILE + n_new, :]], axis=0)
        sc = _dot(qs, ckt_ref[sl, :].astype(BF16)) + bias[:, 0:n_past]
        sn = _dot_nt(qs, k_ref[p]) + bias[:, n_past:n_past + n_new]
        m = jnp.maximum(jnp.max(sc, axis=1, keepdims=True), jnp.max(sn, axis=1, keepdims=True))
        ec = jnp.exp(sc - m)
        en = jnp.exp(sn - m)
        denom = jnp.sum(ec, axis=1, keepdims=True) + jnp.sum(en, axis=1, keepdims=True)
        o = _dot_nt(ec.astype(BF16), cvt_ref[sl, :].astype(BF16)) + _dot(en.astype(BF16), v_ref[p])
        o = o * (1.0 / denom)
        outs.append(jnp.where(lo, o[:n_new], o[n_new:]).astype(BF16))
    o_all = jnp.concatenate(outs, axis=1)
    o_ref[...] = x_ref[...] + _dot(o_all, wo_ref[...])


def _attn_sample_call(xs, q, k, v, cache_kt, cache_vt, layer, bias2, w_o, *, n_dec, n_new, row0, alias):
    d = xs.shape[1]
    n_pairs = q.shape[0]
    n_past = cache_kt.shape[3]
    blk0 = row0 // n_new
    pm_spec = pl.BlockSpec((n_pairs, n_new, LANES), lambda b: (0, blk0 + b, 0))
    cache_spec = pl.BlockSpec((None, None, d, n_past), lambda b: (layer, b, 0, 0))
    row_spec = pl.BlockSpec((n_new, d), lambda b: (b, 0))
    return pl.pallas_call(
        _attn_sample_kernel,
        grid=(n_dec,),
        in_specs=[pm_spec, pm_spec, pm_spec, cache_spec, cache_spec,
                  _resident(bias2.shape), row_spec, _resident(w_o.shape)],
        out_specs=row_spec,
        out_shape=jax.ShapeDtypeStruct(xs.shape, F32),
        input_output_aliases={6: 0} if alias else {},
        compiler_params=_params(("arbitrary",)),
        name="attn_sample",
    )(q, k, v, cache_kt, cache_vt, bias2, xs, w_o)


def _glu(x, g, w_ref, b_ref):
    d = x.shape[1]
    a = _dot(_rms(x, g).astype(BF16), w_ref[...]) + b_ref[...]
    return a[:, :d] * _sigmoid(a[:, d:])


def _fill_ctx(ctx_ref, halo, u):
    for c in range(ctx_ref.shape[0]):
        sl = slice(c * LANES, (c + 1) * LANES)
        ctx_ref[c, 0:HALO_ROWS, :] = halo[:, sl]
        ctx_ref[c, HALO_ROWS:, :] = u[:, sl]


def _depthwise_block(ctx_ref, y_ref, dw_ref, dwb_ref, r0, rb, slabs=None):
    conv_w = dw_ref.shape[0]
    first = HALO_ROWS - (conv_w - 1)
    per_phase = rb // CONV_ROW_STRIDE
    for c in (range(y_ref.shape[0]) if slabs is None else slabs):
        sl = slice(c * LANES, (c + 1) * LANES)
        win = ctx_ref.at[c, pl.ds(r0, rb + HALO_ROWS), :]
        out = y_ref.at[c, pl.ds(r0, rb), :]
        for ph in range(CONV_ROW_STRIDE):
            acc = jnp.zeros((per_phase, LANES), F32)
            for k in range(conv_w):
                acc = acc + dw_ref[k:k + 1, sl] * win[pl.ds(first + k + ph, per_phase, stride=CONV_ROW_STRIDE), :]
            out[pl.ds(ph, per_phase, stride=CONV_ROW_STRIDE), :] = acc + dwb_ref[:, sl]


def _depthwise_steps(ctx_ref, y_ref, dw_ref, dwb_ref):
    n_slabs, n_rows, _ = y_ref.shape
    rb = min(CONV_ROW_BLOCK, n_rows)
    groups = [range(g, g + CONV_STEP_SLABS) for g in range(0, n_slabs, CONV_STEP_SLABS)]
    return [functools.partial(_depthwise_block, ctx_ref, y_ref, dw_ref, dwb_ref, i * rb, rb, slabs)
            for i in range(n_rows // rb) for slabs in groups]


def _emit_interleaved(*stages):
    for k in range(max(len(steps) for steps in stages)):
        for steps in stages:
            if k < len(steps):
                steps[k]()


def _depthwise(ctx_ref, y_ref, dw_ref, dwb_ref):
    n_rows = y_ref.shape[1]
    rb = min(CONV_ROW_BLOCK, n_rows)

    def body(i, carry):
        _depthwise_block(ctx_ref, y_ref, dw_ref, dwb_ref, pl.multiple_of(i * rb, rb), rb)
        return carry

    lax.fori_loop(0, n_rows // rb, body, 0)


def _conv_post(y_ref, x, lng_ref, lnb_ref, w2_ref, b2_ref):
    y = jnp.concatenate([y_ref[c] for c in range(y_ref.shape[0])], axis=1)
    yc = y - jnp.mean(y, axis=-1, keepdims=True)
    yn = yc * lax.rsqrt(jnp.mean(yc * yc, axis=-1, keepdims=True) + EPS) * lng_ref[...] + lnb_ref[...]
    act = (yn * _sigmoid(yn)).astype(BF16)
    return x + _dot(act, w2_ref[...]) + b2_ref[...]


def _conv_scratch(n_rows, d):
    return [pltpu.VMEM((d // LANES, HALO_ROWS + n_rows, LANES), F32), pltpu.VMEM((d // LANES, n_rows, LANES), F32)]


def _convffn_kernel(x_ref, gmix_ref, w1_ref, b1_ref, dw_ref, dwb_ref, lng_ref, lnb_ref, w2_ref, b2_ref,
                    gffn_ref, win_ref, wout_ref, o_ref, last_ref,
                    ctx_ref, y_ref, carry_ref, xkeep_ref, mid_ref, *, tiles_per_batch, n_tiles):
    s = pl.program_id(0)
    slot = s % 2

    @pl.when(s == 0)
    def _():
        ctx_ref[...] = jnp.zeros_like(ctx_ref)
        xkeep_ref[...] = jnp.zeros_like(xkeep_ref)
        mid_ref[...] = jnp.zeros_like(mid_ref)
        carry_ref[...] = jnp.zeros_like(carry_ref)

    d = x_ref.shape[1]
    d_ff = wout_ref.shape[0]
    st = {}

    def ffn_begin():
        st["xb"] = mid_ref[1 - slot]
        st["hb"] = _rms(st["xb"], gffn_ref[...]).astype(BF16)
        st["acc"] = st["xb"]

    def ffn_chunk(lo):
        hi = lo + MXU_DIM
        gate = _dot(st["hb"], win_ref[:, lo:hi])
        up = _dot(st["hb"], win_ref[:, d_ff + lo:d_ff + hi])
        a = (gate * _sigmoid(gate) * up).astype(BF16)
        st["acc"] = st["acc"] + _dot(a, wout_ref[lo:hi, :])

    def ffn_end():
        o_ref[...] = st["acc"]

    prev_ctx = ctx_ref.at[1 - slot]

    def conv_post():
        mid_ref[slot] = _conv_post(y_ref, xkeep_ref[1 - slot], lng_ref, lnb_ref, w2_ref, b2_ref)

    cur_ctx = ctx_ref.at[slot]

    def glu_begin():
        st["x"] = x_ref[...]
        st["h"] = _rms(st["x"], gmix_ref[...]).astype(BF16)
        xkeep_ref[slot] = st["x"]

    def glu_chunk(lo):
        hi = lo + MXU_DIM
        val = _dot(st["h"], w1_ref[:, lo:hi]) + b1_ref[:, lo:hi]
        gate = _dot(st["h"], w1_ref[:, d + lo:d + hi]) + b1_ref[:, d + lo:d + hi]
        u = val * _sigmoid(gate)
        halo = carry_ref[:, lo:hi]
        halo = jnp.where(s % tiles_per_batch == 0, jnp.zeros_like(halo), halo)
        for j in range(MXU_DIM // LANES):
            c = lo // LANES + j
            cur_ctx[c, 0:HALO_ROWS, :] = halo[:, j * LANES:(j + 1) * LANES]
            cur_ctx[c, HALO_ROWS:, :] = u[:, j * LANES:(j + 1) * LANES]
        carry_ref[:, lo:hi] = u[ROW_TILE - HALO_ROWS:, :]

    ffn_steps = [ffn_begin] + [functools.partial(ffn_chunk, lo) for lo in range(0, d_ff, MXU_DIM)] + [ffn_end]
    conv_steps = _depthwise_steps(prev_ctx, y_ref, dw_ref, dwb_ref) + [conv_post]
    glu_steps = [glu_begin] + [functools.partial(glu_chunk, lo) for lo in range(0, d, MXU_DIM)]
    _emit_interleaved(ffn_steps, conv_steps, glu_steps)

    @pl.when(jnp.logical_and(s < n_tiles, s % tiles_per_batch == tiles_per_batch - 1))
    def _():
        last_ref[...] = carry_ref[...]


def _convffn_call(xp, gmix, w1, b1, dw, dwb, lng, lnb, w2, b2, gffn, w_in, w_out, *, n_batch, tiles_per_batch):
    d = xp.shape[1]
    n_tiles = n_batch * tiles_per_batch
    assert xp.shape[0] == n_tiles * ROW_TILE and w_out.shape[0] % MXU_DIM == 0
    vec = _resident((1, d))
    ctx, y = _conv_scratch(ROW_TILE, d)
    return pl.pallas_call(
        functools.partial(_convffn_kernel, tiles_per_batch=tiles_per_batch, n_tiles=n_tiles),
        grid=(n_tiles + 2,),
        in_specs=[pl.BlockSpec((ROW_TILE, d), lambda s: (jnp.minimum(s, n_tiles - 1), 0)),
                  vec, _resident(w1.shape), _resident((1, 2 * d)), _resident(dw.shape), vec, vec, vec,
                  _resident(w2.shape), vec, vec, _resident(w_in.shape), _resident(w_out.shape)],
        out_specs=[pl.BlockSpec((ROW_TILE, d), lambda s: (jnp.maximum(s - 2, 0), 0)),
                   pl.BlockSpec((HALO_ROWS, d), lambda s: (jnp.minimum(s, n_tiles - 1) // tiles_per_batch, 0))],
        out_shape=[jax.ShapeDtypeStruct(xp.shape, F32), jax.ShapeDtypeStruct((n_batch * HALO_ROWS, d), F32)],
        scratch_shapes=[pltpu.VMEM((2,) + ctx.shape, F32), y, pltpu.VMEM((HALO_ROWS, d), F32),
                        pltpu.VMEM((2, ROW_TILE, d), F32), pltpu.VMEM((2, ROW_TILE, d), F32)],
        input_output_aliases={0: 0},
        compiler_params=_params(("arbitrary",)),
        name="conv_ffn_prompt",
    )(xp, gmix, w1, b1, dw, dwb, lng, lnb, w2, b2, gffn, w_in, w_out)


def _glu_kernel(x_ref, g_ref, w_ref, b_ref, u_ref):
    u_ref[...] = _glu(x_ref[...], g_ref[...], w_ref, b_ref)


def _glu_call(x, g, w, b):
    d = x.shape[1]
    row_spec = pl.BlockSpec((ROW_TILE, d), lambda i: (i, 0))
    return pl.pallas_call(
        _glu_kernel,
        grid=(x.shape[0] // ROW_TILE,),
        in_specs=[row_spec, _resident((1, d)), _resident(w.shape), _resident((1, 2 * d))],
        out_specs=row_spec,
        out_shape=jax.ShapeDtypeStruct(x.shape, F32),
        compiler_params=_params(("arbitrary",)),
        name="conv_glu_sample",
    )(x, g, w, b)


def _conv_sample_kernel(state_ref, u_ref, x_ref, dw_ref, dwb_ref, lng_ref, lnb_ref, w2_ref, b2_ref, o_ref,
                        ctx_ref, y_ref):
    _fill_ctx(ctx_ref, state_ref[...], u_ref[...])
    _depthwise(ctx_ref, y_ref, dw_ref, dwb_ref)
    o_ref[...] = _conv_post(y_ref, x_ref[...], lng_ref, lnb_ref, w2_ref, b2_ref)


def _conv_sample_call(xs, u, state, layer, dw, dwb, lng, lnb, w2, b2, *, n_dec, n_new):
    d = xs.shape[1]
    row_spec = pl.BlockSpec((n_new, d), lambda b: (b, 0))
    state_spec = pl.BlockSpec((None, None, HALO_ROWS, d), lambda b: (layer, b, 0, 0))
    return pl.pallas_call(
        _conv_sample_kernel,
        grid=(n_dec,),
        in_specs=[state_spec, row_spec, row_spec, _resident(dw.shape), _resident((1, d)), _resident((1, d)),
                  _resident((1, d)), _resident(w2.shape), _resident((1, d))],
        out_specs=row_spec,
        out_shape=jax.ShapeDtypeStruct(xs.shape, F32),
        scratch_shapes=_conv_scratch(n_new, d),
        input_output_aliases={2: 0},
        compiler_params=_params(("arbitrary",)),
        name="conv_sample",
    )(state, u, xs, dw, dwb, lng, lnb, w2, b2)


def _ffn_rows_kernel(x_ref, g_ref, win_ref, wout_ref, o_ref):
    o_ref[...] = _swiglu(x_ref[...], g_ref[...], win_ref, wout_ref)


def _ffn_rows_call(x, g, w_in, w_out):
    d = x.shape[1]
    row_spec = pl.BlockSpec((ROW_TILE, d), lambda i: (i, 0))
    return pl.pallas_call(
        _ffn_rows_kernel,
        grid=(x.shape[0] // ROW_TILE,),
        in_specs=[row_spec, _resident((1, d)), _resident(w_in.shape), _resident(w_out.shape)],
        out_specs=row_spec,
        out_shape=jax.ShapeDtypeStruct(x.shape, F32),
        input_output_aliases={0: 0},
        compiler_params=_params(("arbitrary",)),
        name="ffn_sample",
    )(x, g, w_in, w_out)


def kernel(x_prompt, x_sample, cache_k, cache_v, state_conv, norm_mix, norm_ffn, w_qkv, q_norm, k_norm,
           rel_table, w_o, pw1_w, pw1_b, dw_w, dw_b, conv_ln_g, conv_ln_b, pw2_w, pw2_b, ffn_w_in, ffn_w_out):
    n_batch, seq, d = x_prompt.shape
    n_dec, n_new, _ = x_sample.shape
    depth = norm_mix.shape[0]
    head_dim = q_norm.shape[1]
    n_heads = d // head_dim
    conv_w = dw_w.shape[1]
    n_prompt = n_batch * seq
    n_sample = n_dec * n_new
    keep = min(BAND_PAST, seq)
    tiles_per_batch = seq // ROW_TILE
    assert seq % ROW_TILE == 0 and n_sample % ROW_TILE == 0 and keep == ROW_TILE
    assert LANES == 2 * head_dim and d % MXU_DIM == 0 and n_new % 8 == 0
    assert conv_w - 1 <= HALO_ROWS and n_new >= conv_w - 1
    assert cache_k.shape[2] == BAND_PAST and cache_k.shape[1] == n_dec

    xp = x_prompt.reshape(n_prompt, d)
    xs = x_sample.reshape(n_sample, d)

    blk = jnp.arange(MXU_DIM) // head_dim
    gmat = jnp.where(blk[:, None] == blk[None, :], 1.0 / head_dim, 0.0).astype(BF16)
    inv_sqrt_dh = float(head_dim) ** -0.5
    cache_kt = jnp.transpose(cache_k, (0, 1, 3, 4, 2)).reshape(cache_k.shape[0], n_dec, d, BAND_PAST)
    cache_vt = jnp.transpose(cache_v, (0, 1, 3, 4, 2)).reshape(cache_v.shape[0], n_dec, d, BAND_PAST)
    state_pad = jnp.pad(state_conv, ((0, 0), (0, 0), (HALO_ROWS - (conv_w - 1), 0), (0, 0)))

    def heads_last(t):
        return jnp.transpose(t.reshape(t.shape[0], n_heads, head_dim, t.shape[2]), (0, 3, 1, 2))

    kp_new, vp_new, ks_new, vs_new, cp_new, cs_new = [], [], [], [], [], []
    for layer in range(depth):
        g_mix = norm_mix[layer].reshape(1, d)
        g_ffn = norm_ffn[layer].reshape(1, d)
        w_in, w_out = ffn_w_in[layer].astype(BF16), ffn_w_out[layer].astype(BF16)
        own = layer > 0
        if layer % 2 == 0:
            a = layer // 2
            gq = (jnp.tile(q_norm[a], n_heads) * inv_sqrt_dh).reshape(1, d)
            gk = jnp.tile(k_norm[a], n_heads).reshape(1, d)
            q, k, v, kt, vt, ksf, vsf = _qkv_call(xp, xs, g_mix, w_qkv[a].astype(BF16), gq, gk, gmat,
                                                  n_batch=n_batch, tiles_per_batch=tiles_per_batch)
            bias = _bias_call(rel_table[a])
            bias2 = bias.reshape(n_heads // 2, 2 * Q_TILE, Q_TILE + BAND_PAST)
            wo = w_o[a].astype(BF16)
            xp = _attn_call(xp, q, k, v, bias2, wo, n_batch=n_batch, seq=seq, alias=own)
            xs = _attn_sample_call(xs, q, k, v, cache_kt, cache_vt, a, bias2, wo,
                                   n_dec=n_dec, n_new=n_new, row0=n_prompt, alias=own)
            kp_new.append(heads_last(kt))
            vp_new.append(heads_last(vt))
            ks_new.append(ksf.reshape(n_dec, n_new, n_heads, head_dim))
            vs_new.append(vsf.reshape(n_dec, n_new, n_heads, head_dim))
            xp, xs = _ffn_call(xp, xs, g_ffn, w_in, w_out, alias=True)
        else:
            c = layer // 2
            glu_args = (g_mix, pw1_w[c].astype(BF16), pw1_b[c].reshape(1, 2 * d))
            conv_args = (dw_w[c], dw_b[c].reshape(1, d), conv_ln_g[c].reshape(1, d), conv_ln_b[c].reshape(1, d),
                         pw2_w[c].astype(BF16), pw2_b[c].reshape(1, d))
            xp, u_last = _convffn_call(xp, *glu_args, *conv_args, g_ffn, w_in, w_out,
                                       n_batch=n_batch, tiles_per_batch=tiles_per_batch)
            u_s = _glu_call(xs, *glu_args)
            xs = _conv_sample_call(xs, u_s, state_pad, c, *conv_args, n_dec=n_dec, n_new=n_new)
            xs = _ffn_rows_call(xs, g_ffn, w_in, w_out)
            cp_new.append(u_last.reshape(n_batch, HALO_ROWS, d)[:, HALO_ROWS - (conv_w - 1):])
            cs_new.append(u_s.reshape(n_dec, n_new, d)[:, n_new - (conv_w - 1):])

    return (xp.reshape(n_batch, seq, d), xs.reshape(n_dec, n_new, d),
            jnp.stack(kp_new), jnp.stack(vp_new), jnp.stack(ks_new), jnp.stack(vs_new),
            jnp.stack(cp_new), jnp.stack(cs_new))
```

```python
import functools

import jax
import jax.numpy as jnp
from jax import lax
from jax.experimental import pallas as pl
from jax.experimental.pallas import tpu as pltpu

CHUNK = 64
PAST_CHUNKS = 8
BAND_PAST = PAST_CHUNKS * CHUNK
EPS = 1e-6
NEG_INF = -1e30

LANES = 128
MXU_DIM = 256
VMEM_LIMIT_BYTES = 58 * 1024 * 1024

ROW_TILE = 512
Q_TILE = 256
ATTN_TILE = 512
CONV_ROW_BLOCK = 32
CONV_ROW_STRIDE = 2
CONV_STEP_SLABS = 8
HALO_ROWS = 32

BF16 = jnp.bfloat16
F32 = jnp.float32


def _dot(a, b):
    return jnp.dot(a, b, preferred_element_type=F32)


def _dot_nt(a, b):
    return lax.dot_general(a, b, (((1,), (1,)), ((), ())), preferred_element_type=F32)


def _rms(x, g):
    ms = jnp.mean(x * x, axis=-1, keepdims=True)
    return x * lax.rsqrt(ms + EPS) * g


def _sigmoid(x):
    return 1.0 / (1.0 + jnp.exp(-x))


def _fold(op, x):
    acc = x[:, 0:LANES]
    for j in range(1, x.shape[1] // LANES):
        acc = op(acc, x[:, j * LANES:(j + 1) * LANES])
    return acc


def _resident(shape):
    nd = len(shape)
    return pl.BlockSpec(shape, lambda *_: (0,) * nd, pipeline_mode=pl.Buffered(1))


class _Layer:
    def __init__(self, stack, index):
        self.stack, self.index, self.shape = stack, index, stack.shape[1:]


def _wspec(w):
    if not isinstance(w, _Layer):
        return _resident(w.shape)
    tail = (0,) * len(w.shape)
    return pl.BlockSpec((None,) + w.shape, lambda *_: (w.index,) + tail, pipeline_mode=pl.Buffered(1))


def _warg(w):
    return w.stack if isinstance(w, _Layer) else w


def _params(sem):
    return pltpu.CompilerParams(dimension_semantics=sem, vmem_limit_bytes=VMEM_LIMIT_BYTES)


def _split_specs(np_t, d):
    return (pl.BlockSpec((ROW_TILE, d), lambda i: (jnp.minimum(i, np_t - 1), 0)),
            pl.BlockSpec((ROW_TILE, d), lambda i: (jnp.maximum(i - np_t, 0), 0)))


def _load_rows(i, np_t, xp_ref, xs_ref):
    return jnp.where(i < np_t, xp_ref[...], xs_ref[...])


def _store_rows(i, np_t, y, op_ref, os_ref):
    @pl.when(i < np_t)
    def _():
        op_ref[...] = y

    @pl.when(i >= np_t)
    def _():
        os_ref[...] = y


def _swiglu(x, g, win_ref, wout_ref):
    d_ff = wout_ref.shape[0]
    n_tiles = d_ff // MXU_DIM
    bounds = (0, (n_tiles + 1) // 2 * MXU_DIM, d_ff)
    h = _rms(x, g).astype(BF16)
    acc = x
    for lo, hi in zip(bounds[:-1], bounds[1:]):
        gate = _dot(h, win_ref[:, lo:hi])
        up = _dot(h, win_ref[:, d_ff + lo:d_ff + hi])
        a = (gate * _sigmoid(gate) * up).astype(BF16)
        acc = acc + _dot(a, wout_ref[lo:hi, :])
    return acc


def _ffn_kernel(xp_ref, xs_ref, g_ref, win_ref, wout_ref, op_ref, os_ref, *, np_t):
    i = pl.program_id(0)
    x = _load_rows(i, np_t, xp_ref, xs_ref)
    _store_rows(i, np_t, _swiglu(x, g_ref[...], win_ref, wout_ref), op_ref, os_ref)


def _ffn_call(xp, xs, g, w_in, w_out, *, alias):
    d = xp.shape[1]
    assert w_out.shape[0] % MXU_DIM == 0
    np_t, ns_t = xp.shape[0] // ROW_TILE, xs.shape[0] // ROW_TILE
    p_spec, s_spec = _split_specs(np_t, d)
    return pl.pallas_call(
        functools.partial(_ffn_kernel, np_t=np_t),
        grid=(np_t + ns_t,),
        in_specs=[p_spec, s_spec, _resident((1, d)), _wspec(w_in), _wspec(w_out)],
        out_specs=[p_spec, s_spec],
        out_shape=[jax.ShapeDtypeStruct(xp.shape, F32), jax.ShapeDtypeStruct(xs.shape, F32)],
        input_output_aliases={0: 0, 1: 1} if alias else {},
        compiler_params=_params(("arbitrary",)),
        name="ffn",
    )(xp, xs, g, _warg(w_in), _warg(w_out))


def _qkv_kernel(xp_ref, xs_ref, g_ref, w_ref, gq_ref, gk_ref, gmat_ref,
                q_ref, k_ref, v_ref, kt_ref, vt_ref, ks_ref, vs_ref, *, np_t, tiles_per_batch):
    i = pl.program_id(0)
    d = xp_ref.shape[1]
    h = _rms(_load_rows(i, np_t, xp_ref, xs_ref), g_ref[...]).astype(BF16)
    qkv = _dot(h, w_ref[...])
    gmat = gmat_ref[...]

    def head_norm(t, g):
        sq = (t * t).astype(BF16)
        ms = jnp.concatenate([_dot(sq[:, j * MXU_DIM:(j + 1) * MXU_DIM], gmat)
                              for j in range(d // MXU_DIM)], axis=1)
        return t * lax.rsqrt(ms + EPS) * g

    qn = head_norm(qkv[:, :d], gq_ref[...])
    kn = head_norm(qkv[:, d:2 * d], gk_ref[...])
    v = qkv[:, 2 * d:]
    for p in range(d // LANES):
        sl = slice(p * LANES, (p + 1) * LANES)
        q_ref[p] = qn[:, sl].astype(BF16)
        k_ref[p] = kn[:, sl].astype(BF16)
        v_ref[p] = v[:, sl].astype(BF16)

    @pl.when(jnp.logical_and(i < np_t, i % tiles_per_batch == tiles_per_batch - 1))
    def _():
        kt_ref[...] = kn.T
        vt_ref[...] = v.T

    @pl.when(i >= np_t)
    def _():
        ks_ref[...] = kn
        vs_ref[...] = v


def _qkv_call(xp, xs, g, w, gq, gk, gmat, *, n_batch, tiles_per_batch):
    d = xp.shape[1]
    np_t, ns_t = xp.shape[0] // ROW_TILE, xs.shape[0] // ROW_TILE
    n_rows = xp.shape[0] + xs.shape[0]
    n_pairs = d // LANES
    p_spec, s_spec = _split_specs(np_t, d)
    pm_spec = pl.BlockSpec((n_pairs, ROW_TILE, LANES), lambda i: (0, i, 0))
    pm_shape = jax.ShapeDtypeStruct((n_pairs, n_rows, LANES), BF16)
    t_spec = pl.BlockSpec((None, d, ROW_TILE), lambda i: (jnp.minimum(i // tiles_per_batch, n_batch - 1), 0, 0))
    t_shape = jax.ShapeDtypeStruct((n_batch, d, ROW_TILE), F32)
    return pl.pallas_call(
        functools.partial(_qkv_kernel, np_t=np_t, tiles_per_batch=tiles_per_batch),
        grid=(np_t + ns_t,),
        in_specs=[p_spec, s_spec, _resident((1, d)), _wspec(w), _resident((1, d)), _resident((1, d)),
                  _resident(gmat.shape)],
        out_specs=[pm_spec, pm_spec, pm_spec, t_spec, t_spec, s_spec, s_spec],
        out_shape=[pm_shape, pm_shape, pm_shape, t_shape, t_shape,
                   jax.ShapeDtypeStruct(xs.shape, F32), jax.ShapeDtypeStruct(xs.shape, F32)],
        compiler_params=_params(("arbitrary",)),
        name="qkv",
    )(xp, xs, g, _warg(w), gq, gk, gmat)


def _bias_kernel(r_ref, o_ref):
    n_keys = o_ref.shape[2]
    width = r_ref.shape[2]
    x = jnp.broadcast_to(r_ref[0], (Q_TILE, width))
    row = lax.broadcasted_iota(jnp.int32, (Q_TILE, width), 0)
    shift = 1
    while shift < Q_TILE:
        x = jnp.where((row & shift) != 0, pltpu.roll(x, shift, 1), x)
        shift *= 2
    x = x[:, :n_keys]
    qc = lax.broadcasted_iota(jnp.int32, (Q_TILE, n_keys), 0) // CHUNK
    kc = lax.broadcasted_iota(jnp.int32, (Q_TILE, n_keys), 1) // CHUNK
    in_band = jnp.logical_and(kc >= qc, kc <= qc + PAST_CHUNKS)
    o_ref[0] = jnp.where(in_band, x, NEG_INF)


def _bias_call(rel_table):
    n_heads, n_rel = rel_table.shape
    clip = (n_rel - 1) // 2
    n_keys = Q_TILE + BAND_PAST
    width = 1024
    assert width >= n_keys + Q_TILE and BAND_PAST >= clip
    far = jnp.broadcast_to(rel_table[:, n_rel - 1:], (n_heads, BAND_PAST - clip))
    near = jnp.broadcast_to(rel_table[:, :1], (n_heads, n_keys - (BAND_PAST + clip) - 1))
    wrap = jnp.broadcast_to(rel_table[:, n_rel - 1:], (n_heads, width - n_keys))
    gen = jnp.concatenate([far, rel_table[:, ::-1], near, wrap], axis=1).reshape(n_heads, 1, width)
    return pl.pallas_call(
        _bias_kernel,
        grid=(n_heads,),
        in_specs=[pl.BlockSpec((1, 1, width), lambda h: (h, 0, 0))],
        out_specs=pl.BlockSpec((1, Q_TILE, n_keys), lambda h: (h, 0, 0)),
        out_shape=jax.ShapeDtypeStruct((n_heads, Q_TILE, n_keys), F32),
        compiler_params=_params(("arbitrary",)),
        name="rel_bias",
    )(gen)


def _split_heads(qp, lo):
    zero = jnp.zeros_like(qp)
    return jnp.concatenate([jnp.where(lo, qp, zero), jnp.where(lo, zero, qp)], axis=0)


def _attn_kernel(q_ref, kp_ref, kc_ref, vp_ref, vc_ref, bias_ref, x_ref, wo_ref, o_ref, opair_ref):
    t = pl.program_id(1)
    n_pairs = q_ref.shape[0]
    blocks_per_tile = ATTN_TILE // Q_TILE
    pen_prev = jnp.where(t < 1, NEG_INF, 0.0).astype(F32)
    lo = lax.broadcasted_iota(jnp.int32, (Q_TILE, LANES), 1) < LANES // 2

    for g in range(blocks_per_tile):
        rows = slice(g * Q_TILE, (g + 1) * Q_TILE)
        for p in range(n_pairs):
            qs = _split_heads(q_ref[p, rows, :], lo)
            s, pens, vals = [], [], []
            for j in range(BAND_PAST // Q_TILE + 1):
                blk = g + j
                from_prev = blk < blocks_per_tile
                k_ref, v_ref = (kp_ref, vp_ref) if from_prev else (kc_ref, vc_ref)
                krows = slice((blk % blocks_per_tile) * Q_TILE, (blk % blocks_per_tile + 1) * Q_TILE)
                s.append(_dot_nt(qs, k_ref[p, krows, :]) + bias_ref[p, :, j * Q_TILE:(j + 1) * Q_TILE])
                pens.append(pen_prev if from_prev else None)
                vals.append(v_ref[p, krows, :])
            mx = None
            for sj, pen in zip(s, pens):
                fj = _fold(jnp.maximum, sj)
                fj = fj if pen is None else fj + pen
                mx = fj if mx is None else jnp.maximum(mx, fj)
            m = jnp.max(mx, axis=1, keepdims=True)
            e = [jnp.exp(sj - (m if pen is None else m - pen)) for sj, pen in zip(s, pens)]
            denom = jnp.sum(sum(_fold(jnp.add, ej) for ej in e), axis=1, keepdims=True)
            o = sum(_dot(ej.astype(BF16), vj) for ej, vj in zip(e, vals))
            o = o * (1.0 / denom)
            opair_ref[p, rows, :] = jnp.where(lo, o[:Q_TILE], o[Q_TILE:]).astype(BF16)
    o_all = jnp.concatenate([opair_ref[p] for p in range(n_pairs)], axis=1)
    o_ref[...] = x_ref[...] + _dot(o_all, wo_ref[...])


def _attn_call(xp, q, k, v, bias2, w_o, *, n_batch, seq, alias):
    d = xp.shape[1]
    n_pairs = q.shape[0]
    nt = seq // ATTN_TILE
    assert ATTN_TILE == BAND_PAST and ATTN_TILE % Q_TILE == 0 and seq % ATTN_TILE == 0

    def blk(back):
        return pl.BlockSpec((n_pairs, ATTN_TILE, LANES),
                            lambda b, t: (0, b * nt + jnp.maximum(t - back, 0), 0))

    row_spec = pl.BlockSpec((ATTN_TILE, d), lambda b, t: (b * nt + t, 0))
    return pl.pallas_call(
        _attn_kernel,
        grid=(n_batch, nt),
        in_specs=[blk(0), blk(1), blk(0), blk(1), blk(0),
                  _wspec(bias2), row_spec, _wspec(w_o)],
        out_specs=row_spec,
        out_shape=jax.ShapeDtypeStruct(xp.shape, F32),
        scratch_shapes=[pltpu.VMEM((n_pairs, ATTN_TILE, LANES), BF16)],
        input_output_aliases={6: 0} if alias else {},
        compiler_params=_params(("arbitrary", "arbitrary")),
        name="attn_prompt",
    )(q, k, k, v, v, _warg(bias2), xp, _warg(w_o))


def _attn_sample_kernel(q_ref, k_ref, v_ref, ckt_ref, cvt_ref, bias_ref, x_ref, wo_ref, o_ref):
    n_pairs, n_new, _ = q_ref.shape
    n_past = ckt_ref.shape[1]
    lo = lax.broadcasted_iota(jnp.int32, (n_new, LANES), 1) < LANES // 2
    outs = []
    for p in range(n_pairs):
        sl = slice(p * LANES, (p + 1) * LANES)
        qs = _split_heads(q_ref[p], lo)
        bias = jnp.concatenate([bias_ref[p, 0:n_new, :], bias_ref[p, Q_TILE:Q_TILE + n_new, :]], axis=0)
        sc = _dot(qs, ckt_ref[sl, :].astype(BF16)) + bias[:, 0:n_past]
        sn = _dot_nt(qs, k_ref[p]) + bias[:, n_past:n_past + n_new]
        m = jnp.maximum(jnp.max(sc, axis=1, keepdims=True), jnp.max(sn, axis=1, keepdims=True))
        ec = jnp.exp(sc - m)
        en = jnp.exp(sn - m)
        denom = jnp.sum(ec, axis=1, keepdims=True) + jnp.sum(en, axis=1, keepdims=True)
        o = _dot_nt(ec.astype(BF16), cvt_ref[sl, :].astype(BF16)) + _dot(en.astype(BF16), v_ref[p])
        o = o * (1.0 / denom)
        outs.append(jnp.where(lo, o[:n_new], o[n_new:]).astype(BF16))
    o_all = jnp.concatenate(outs, axis=1)
    o_ref[...] = x_ref[...] + _dot(o_all, wo_ref[...])


def _attn_sample_call(xs, q, k, v, cache_kt, cache_vt, layer, bias2, w_o, *, n_dec, n_new, row0, alias):
    d = xs.shape[1]
    n_pairs = q.shape[0]
    n_past = cache_kt.shape[3]
    blk0 = row0 // n_new
    pm_spec = pl.BlockSpec((n_pairs, n_new, LANES), lambda b: (0, blk0 + b, 0))
    cache_spec = pl.BlockSpec((None, None, d, n_past), lambda b: (layer, b, 0, 0))
    row_spec = pl.BlockSpec((n_new, d), lambda b: (b, 0))
    return pl.pallas_call(
        _attn_sample_kernel,
        grid=(n_dec,),
        in_specs=[pm_spec, pm_spec, pm_spec, cache_spec, cache_spec,
                  _wspec(bias2), row_spec, _wspec(w_o)],
        out_specs=row_spec,
        out_shape=jax.ShapeDtypeStruct(xs.shape, F32),
        input_output_aliases={6: 0} if alias else {},
        compiler_params=_params(("arbitrary",)),
        name="attn_sample",
    )(q, k, v, cache_kt, cache_vt, _warg(bias2), xs, _warg(w_o))


def _glu(x, g, w_ref, b_ref):
    d = x.shape[1]
    a = _dot(_rms(x, g).astype(BF16), w_ref[...]) + b_ref[...]
    return a[:, :d] * _sigmoid(a[:, d:])


def _fill_ctx(ctx_ref, halo, u):
    for c in range(ctx_ref.shape[0]):
        sl = slice(c * LANES, (c + 1) * LANES)
        ctx_ref[c, 0:HALO_ROWS, :] = halo[:, sl]
        ctx_ref[c, HALO_ROWS:, :] = u[:, sl]


def _depthwise_block(ctx_ref, y_ref, dw_ref, dwb_ref, r0, rb, slabs=None):
    conv_w = dw_ref.shape[0]
    first = HALO_ROWS - (conv_w - 1)
    per_phase = rb // CONV_ROW_STRIDE
    for c in (range(y_ref.shape[0]) if slabs is None else slabs):
        sl = slice(c * LANES, (c + 1) * LANES)
        win = ctx_ref.at[c, pl.ds(r0, rb + HALO_ROWS), :]
        out = y_ref.at[c, pl.ds(r0, rb), :]
        for ph in range(CONV_ROW_STRIDE):
            acc = jnp.zeros((per_phase, LANES), F32)
            for k in range(conv_w):
                acc = acc + dw_ref[k:k + 1, sl] * win[pl.ds(first + k + ph, per_phase, stride=CONV_ROW_STRIDE), :]
            out[pl.ds(ph, per_phase, stride=CONV_ROW_STRIDE), :] = acc + dwb_ref[:, sl]


def _depthwise_steps(ctx_ref, y_ref, dw_ref, dwb_ref):
    n_slabs, n_rows, _ = y_ref.shape
    rb = min(CONV_ROW_BLOCK, n_rows)
    groups = [range(g, g + CONV_STEP_SLABS) for g in range(0, n_slabs, CONV_STEP_SLABS)]
    return [functools.partial(_depthwise_block, ctx_ref, y_ref, dw_ref, dwb_ref, i * rb, rb, slabs)
            for i in range(n_rows // rb) for slabs in groups]


def _emit_interleaved(*stages):
    for k in range(max(len(steps) for steps in stages)):
        for steps in stages:
            if k < len(steps):
                steps[k]()


def _depthwise(ctx_ref, y_ref, dw_ref, dwb_ref):
    n_rows = y_ref.shape[1]
    rb = min(CONV_ROW_BLOCK, n_rows)

    def body(i, carry):
        _depthwise_block(ctx_ref, y_ref, dw_ref, dwb_ref, pl.multiple_of(i * rb, rb), rb)
        return carry

    lax.fori_loop(0, n_rows // rb, body, 0)


def _conv_post(y_ref, x, lng_ref, lnb_ref, w2_ref, b2_ref):
    y = jnp.concatenate([y_ref[c] for c in range(y_ref.shape[0])], axis=1)
    yc = y - jnp.mean(y, axis=-1, keepdims=True)
    yn = yc * lax.rsqrt(jnp.mean(yc * yc, axis=-1, keepdims=True) + EPS) * lng_ref[...] + lnb_ref[...]
    act = (yn * _sigmoid(yn)).astype(BF16)
    return x + _dot(act, w2_ref[...]) + b2_ref[...]


def _conv_scratch(n_rows, d):
    return [pltpu.VMEM((d // LANES, HALO_ROWS + n_rows, LANES), F32), pltpu.VMEM((d // LANES, n_rows, LANES), F32)]


def _convffn_kernel(x_ref, gmix_ref, w1_ref, b1_ref, dw_ref, dwb_ref, lng_ref, lnb_ref, w2_ref, b2_ref,
                    gffn_ref, win_ref, wout_ref, o_ref, last_ref,
                    ctx_ref, y_ref, carry_ref, xkeep_ref, mid_ref, *, tiles_per_batch, n_tiles):
    s = pl.program_id(0)
    slot = s % 2

    @pl.when(s == 0)
    def _():
        ctx_ref[...] = jnp.zeros_like(ctx_ref)
        xkeep_ref[...] = jnp.zeros_like(xkeep_ref)
        mid_ref[...] = jnp.zeros_like(mid_ref)
        carry_ref[...] = jnp.zeros_like(carry_ref)

    d = x_ref.shape[1]
    d_ff = wout_ref.shape[0]
    st = {}

    def ffn_begin():
        st["xb"] = mid_ref[1 - slot]
        st["hb"] = _rms(st["xb"], gffn_ref[...]).astype(BF16)
        st["acc"] = st["xb"]

    def ffn_chunk(lo):
        hi = lo + MXU_DIM
        gate = _dot(st["hb"], win_ref[:, lo:hi])
        up = _dot(st["hb"], win_ref[:, d_ff + lo:d_ff + hi])
        a = (gate * _sigmoid(gate) * up).astype(BF16)
        st["acc"] = st["acc"] + _dot(a, wout_ref[lo:hi, :])

    def ffn_end():
        o_ref[...] = st["acc"]

    prev_ctx = ctx_ref.at[1 - slot]

    def conv_post():
        mid_ref[slot] = _conv_post(y_ref, xkeep_ref[1 - slot], lng_ref, lnb_ref, w2_ref, b2_ref)

    cur_ctx = ctx_ref.at[slot]

    def glu_begin():
        st["x"] = x_ref[...]
        st["h"] = _rms(st["x"], gmix_ref[...]).astype(BF16)
        xkeep_ref[slot] = st["x"]

    def glu_chunk(lo):
        hi = lo + MXU_DIM
        val = _dot(st["h"], w1_ref[:, lo:hi]) + b1_ref[:, lo:hi]
        gate = _dot(st["h"], w1_ref[:, d + lo:d + hi]) + b1_ref[:, d + lo:d + hi]
        u = val * _sigmoid(gate)
        halo = carry_ref[:, lo:hi]
        halo = jnp.where(s % tiles_per_batch == 0, jnp.zeros_like(halo), halo)
        for j in range(MXU_DIM // LANES):
            c = lo // LANES + j
            cur_ctx[c, 0:HALO_ROWS, :] = halo[:, j * LANES:(j + 1) * LANES]
            cur_ctx[c, HALO_ROWS:, :] = u[:, j * LANES:(j + 1) * LANES]
        carry_ref[:, lo:hi] = u[ROW_TILE - HALO_ROWS:, :]

    ffn_steps = [ffn_begin] + [functools.partial(ffn_chunk, lo) for lo in range(0, d_ff, MXU_DIM)] + [ffn_end]
    conv_steps = _depthwise_steps(prev_ctx, y_ref, dw_ref, dwb_ref) + [conv_post]
    glu_steps = [glu_begin] + [functools.partial(glu_chunk, lo) for lo in range(0, d, MXU_DIM)]
    _emit_interleaved(ffn_steps, conv_steps, glu_steps)

    @pl.when(jnp.logical_and(s < n_tiles, s % tiles_per_batch == tiles_per_batch - 1))
    def _():
        last_ref[...] = carry_ref[...]


def _convffn_call(xp, gmix, w1, b1, dw, dwb, lng, lnb, w2, b2, gffn, w_in, w_out, *, n_batch, tiles_per_batch):
    d = xp.shape[1]
    n_tiles = n_batch * tiles_per_batch
    assert xp.shape[0] == n_tiles * ROW_TILE and w_out.shape[0] % MXU_DIM == 0
    vec = _resident((1, d))
    ctx, y = _conv_scratch(ROW_TILE, d)
    return pl.pallas_call(
        functools.partial(_convffn_kernel, tiles_per_batch=tiles_per_batch, n_tiles=n_tiles),
        grid=(n_tiles + 2,),
        in_specs=[pl.BlockSpec((ROW_TILE, d), lambda s: (jnp.minimum(s, n_tiles - 1), 0)),
                  vec, _wspec(w1), _resident((1, 2 * d)), _resident(dw.shape), vec, vec, vec,
                  _wspec(w2), vec, vec, _wspec(w_in), _wspec(w_out)],
        out_specs=[pl.BlockSpec((ROW_TILE, d), lambda s: (jnp.maximum(s - 2, 0), 0)),
                   pl.BlockSpec((HALO_ROWS, d), lambda s: (jnp.minimum(s, n_tiles - 1) // tiles_per_batch, 0))],
        out_shape=[jax.ShapeDtypeStruct(xp.shape, F32), jax.ShapeDtypeStruct((n_batch * HALO_ROWS, d), F32)],
        scratch_shapes=[pltpu.VMEM((2,) + ctx.shape, F32), y, pltpu.VMEM((HALO_ROWS, d), F32),
                        pltpu.VMEM((2, ROW_TILE, d), F32), pltpu.VMEM((2, ROW_TILE, d), F32)],
        input_output_aliases={0: 0},
        compiler_params=_params(("arbitrary",)),
        name="conv_ffn_prompt",
    )(xp, gmix, _warg(w1), b1, dw, dwb, lng, lnb, _warg(w2), b2, gffn, _warg(w_in), _warg(w_out))


def _glu_kernel(x_ref, g_ref, w_ref, b_ref, u_ref):
    u_ref[...] = _glu(x_ref[...], g_ref[...], w_ref, b_ref)


def _glu_call(x, g, w, b):
    d = x.shape[1]
    row_spec = pl.BlockSpec((ROW_TILE, d), lambda i: (i, 0))
    return pl.pallas_call(
        _glu_kernel,
        grid=(x.shape[0] // ROW_TILE,),
        in_specs=[row_spec, _resident((1, d)), _wspec(w), _resident((1, 2 * d))],
        out_specs=row_spec,
        out_shape=jax.ShapeDtypeStruct(x.shape, F32),
        compiler_params=_params(("arbitrary",)),
        name="conv_glu_sample",
    )(x, g, _warg(w), b)


def _conv_sample_kernel(state_ref, u_ref, x_ref, dw_ref, dwb_ref, lng_ref, lnb_ref, w2_ref, b2_ref, o_ref,
                        ctx_ref, y_ref):
    _fill_ctx(ctx_ref, state_ref[...], u_ref[...])
    _depthwise(ctx_ref, y_ref, dw_ref, dwb_ref)
    o_ref[...] = _conv_post(y_ref, x_ref[...], lng_ref, lnb_ref, w2_ref, b2_ref)


def _conv_sample_call(xs, u, state, layer, dw, dwb, lng, lnb, w2, b2, *, n_dec, n_new):
    d = xs.shape[1]
    row_spec = pl.BlockSpec((n_new, d), lambda b: (b, 0))
    state_spec = pl.BlockSpec((None, None, HALO_ROWS, d), lambda b: (layer, b, 0, 0))
    return pl.pallas_call(
        _conv_sample_kernel,
        grid=(n_dec,),
        in_specs=[state_spec, row_spec, row_spec, _resident(dw.shape), _resident((1, d)), _resident((1, d)),
                  _resident((1, d)), _wspec(w2), _resident((1, d))],
        out_specs=row_spec,
        out_shape=jax.ShapeDtypeStruct(xs.shape, F32),
        scratch_shapes=_conv_scratch(n_new, d),
        input_output_aliases={2: 0},
        compiler_params=_params(("arbitrary",)),
        name="conv_sample",
    )(state, u, xs, dw, dwb, lng, lnb, _warg(w2), b2)


def _ffn_rows_kernel(x_ref, g_ref, win_ref, wout_ref, o_ref):
    o_ref[...] = _swiglu(x_ref[...], g_ref[...], win_ref, wout_ref)


def _ffn_rows_call(x, g, w_in, w_out):
    d = x.shape[1]
    row_spec = pl.BlockSpec((ROW_TILE, d), lambda i: (i, 0))
    return pl.pallas_call(
        _ffn_rows_kernel,
        grid=(x.shape[0] // ROW_TILE,),
        in_specs=[row_spec, _resident((1, d)), _wspec(w_in), _wspec(w_out)],
        out_specs=row_spec,
        out_shape=jax.ShapeDtypeStruct(x.shape, F32),
        input_output_aliases={0: 0},
        compiler_params=_params(("arbitrary",)),
        name="ffn_sample",
    )(x, g, _warg(w_in), _warg(w_out))


def kernel(x_prompt, x_sample, cache_k, cache_v, state_conv, norm_mix, norm_ffn, w_qkv, q_norm, k_norm,
           rel_table, w_o, pw1_w, pw1_b, dw_w, dw_b, conv_ln_g, conv_ln_b, pw2_w, pw2_b, ffn_w_in, ffn_w_out):
    n_batch, seq, d = x_prompt.shape
    n_dec, n_new, _ = x_sample.shape
    depth = norm_mix.shape[0]
    head_dim = q_norm.shape[1]
    n_heads = d // head_dim
    conv_w = dw_w.shape[1]
    n_prompt = n_batch * seq
    n_sample = n_dec * n_new
    keep = min(BAND_PAST, seq)
    tiles_per_batch = seq // ROW_TILE
    assert seq % ROW_TILE == 0 and n_sample % ROW_TILE == 0 and keep == ROW_TILE
    assert LANES == 2 * head_dim and d % MXU_DIM == 0 and n_new % 8 == 0
    assert conv_w - 1 <= HALO_ROWS and n_new >= conv_w - 1
    assert cache_k.shape[2] == BAND_PAST and cache_k.shape[1] == n_dec

    xp = x_prompt.reshape(n_prompt, d)
    xs = x_sample.reshape(n_sample, d)

    blk = jnp.arange(MXU_DIM) // head_dim
    gmat = jnp.where(blk[:, None] == blk[None, :], 1.0 / head_dim, 0.0).astype(BF16)
    inv_sqrt_dh = float(head_dim) ** -0.5
    cache_kt = jnp.transpose(cache_k, (0, 1, 3, 4, 2)).reshape(cache_k.shape[0], n_dec, d, BAND_PAST)
    cache_vt = jnp.transpose(cache_v, (0, 1, 3, 4, 2)).reshape(cache_v.shape[0], n_dec, d, BAND_PAST)
    state_pad = jnp.pad(state_conv, ((0, 0), (0, 0), (HALO_ROWS - (conv_w - 1), 0), (0, 0)))

    def heads_last(t):
        return jnp.transpose(t.reshape(t.shape[0], n_heads, head_dim, t.shape[2]), (0, 3, 1, 2))

    ffn_in_bf, ffn_out_bf, qkv_bf, wo_bf, pw1_bf, pw2_bf = (
        t.astype(BF16) for t in (ffn_w_in, ffn_w_out, w_qkv, w_o, pw1_w, pw2_w))
    n_attn = rel_table.shape[0]
    bias_all = _bias_call(rel_table.reshape(n_attn * n_heads, rel_table.shape[2])).reshape(
        n_attn, n_heads // 2, 2 * Q_TILE, Q_TILE + BAND_PAST)

    kp_new, vp_new, ks_new, vs_new, cp_new, cs_new = [], [], [], [], [], []
    for layer in range(depth):
        g_mix = norm_mix[layer].reshape(1, d)
        g_ffn = norm_ffn[layer].reshape(1, d)
        w_in, w_out = _Layer(ffn_in_bf, layer), _Layer(ffn_out_bf, layer)
        own = layer > 0
        if layer % 2 == 0:
            a = layer // 2
            gq = (jnp.tile(q_norm[a], n_heads) * inv_sqrt_dh).reshape(1, d)
            gk = jnp.tile(k_norm[a], n_heads).reshape(1, d)
            q, k, v, kt, vt, ksf, vsf = _qkv_call(xp, xs, g_mix, _Layer(qkv_bf, a), gq, gk, gmat,
                                                  n_batch=n_batch, tiles_per_batch=tiles_per_batch)
            bias2 = _Layer(bias_all, a)
            wo = _Layer(wo_bf, a)
            xp = _attn_call(xp, q, k, v, bias2, wo, n_batch=n_batch, seq=seq, alias=own)
            xs = _attn_sample_call(xs, q, k, v, cache_kt, cache_vt, a, bias2, wo,
                                   n_dec=n_dec, n_new=n_new, row0=n_prompt, alias=own)
            kp_new.append(heads_last(kt))
            vp_new.append(heads_last(vt))
            ks_new.append(ksf.reshape(n_dec, n_new, n_heads, head_dim))
            vs_new.append(vsf.reshape(n_dec, n_new, n_heads, head_dim))
            xp, xs = _ffn_call(xp, xs, g_ffn, w_in, w_out, alias=True)
        else:
            c = layer // 2
            glu_args = (g_mix, _Layer(pw1_bf, c), pw1_b[c].reshape(1, 2 * d))
            conv_args = (dw_w[c], dw_b[c].reshape(1, d), conv_ln_g[c].reshape(1, d), conv_ln_b[c].reshape(1, d),
                         _Layer(pw2_bf, c), pw2_b[c].reshape(1, d))
            xp, u_last = _convffn_call(xp, *glu_args, *conv_args, g_ffn, w_in, w_out,
                                       n_batch=n_batch, tiles_per_batch=tiles_per_batch)
            u_s = _glu_call(xs, *glu_args)
            xs = _conv_sample_call(xs, u_s, state_pad, c, *conv_args, n_dec=n_dec, n_new=n_new)
            xs = _ffn_rows_call(xs, g_ffn, w_in, w_out)
            cp_new.append(u_last.reshape(n_batch, HALO_ROWS, d)[:, HALO_ROWS - (conv_w - 1):])
            cs_new.append(u_s.reshape(n_dec, n_new, d)[:, n_new - (conv_w - 1):])

    return (xp.reshape(n_batch, seq, d), xs.reshape(n_dec, n_new, d),
            jnp.stack(kp_new), jnp.stack(vp_new), jnp.stack(ks_new), jnp.stack(vs_new),
            jnp.stack(cp_new), jnp.stack(cs_new))
```

```python
import functools

import jax
import jax.numpy as jnp
from jax import lax
from jax.experimental import pallas as pl
from jax.experimental.pallas import tpu as pltpu

CHUNK = 64
PAST_CHUNKS = 8
BAND_PAST = PAST_CHUNKS * CHUNK
EPS = 1e-6
NEG_INF = -1e30

LANES = 128
MXU_DIM = 256
VMEM_LIMIT_BYTES = 58 * 1024 * 1024

ROW_TILE = 512
Q_TILE = 256
ATTN_TILE = 512
CONV_ROW_BLOCK = 32
CONV_ROW_STRIDE = 2
CONV_STEP_SLABS = 8
HALO_ROWS = 32

BF16 = jnp.bfloat16
F32 = jnp.float32


def _dot(a, b):
    return jnp.dot(a, b, preferred_element_type=F32)


def _dot_nt(a, b):
    return lax.dot_general(a, b, (((1,), (1,)), ((), ())), preferred_element_type=F32)


def _rms(x, g):
    ms = jnp.mean(x * x, axis=-1, keepdims=True)
    return x * lax.rsqrt(ms + EPS) * g


def _sigmoid(x):
    return 1.0 / (1.0 + jnp.exp(-x))


def _fold(op, x):
    acc = x[:, 0:LANES]
    for j in range(1, x.shape[1] // LANES):
        acc = op(acc, x[:, j * LANES:(j + 1) * LANES])
    return acc


def _resident(shape):
    nd = len(shape)
    return pl.BlockSpec(shape, lambda *_: (0,) * nd, pipeline_mode=pl.Buffered(1))


class _Layer:
    def __init__(self, stack, index):
        self.stack, self.index, self.shape = stack, index, stack.shape[1:]


def _wspec(w):
    if not isinstance(w, _Layer):
        return _resident(w.shape)
    tail = (0,) * len(w.shape)
    return pl.BlockSpec((None,) + w.shape, lambda *_: (w.index,) + tail, pipeline_mode=pl.Buffered(1))


def _warg(w):
    return w.stack if isinstance(w, _Layer) else w


def _params(sem):
    return pltpu.CompilerParams(dimension_semantics=sem, vmem_limit_bytes=VMEM_LIMIT_BYTES)


def _split_specs(np_t, d):
    return (pl.BlockSpec((ROW_TILE, d), lambda i: (jnp.minimum(i, np_t - 1), 0)),
            pl.BlockSpec((ROW_TILE, d), lambda i: (jnp.maximum(i - np_t, 0), 0)))


def _load_rows(i, np_t, xp_ref, xs_ref):
    return jnp.where(i < np_t, xp_ref[...], xs_ref[...])


def _store_rows(i, np_t, y, op_ref, os_ref):
    @pl.when(i < np_t)
    def _():
        op_ref[...] = y

    @pl.when(i >= np_t)
    def _():
        os_ref[...] = y


def _swiglu(x, g, win_ref, wout_ref):
    d_ff = wout_ref.shape[0]
    n_tiles = d_ff // MXU_DIM
    bounds = (0, (n_tiles + 1) // 2 * MXU_DIM, d_ff)
    h = _rms(x, g).astype(BF16)
    acc = x
    for lo, hi in zip(bounds[:-1], bounds[1:]):
        gate = _dot(h, win_ref[:, lo:hi])
        up = _dot(h, win_ref[:, d_ff + lo:d_ff + hi])
        a = (gate * _sigmoid(gate) * up).astype(BF16)
        acc = acc + _dot(a, wout_ref[lo:hi, :])
    return acc


def _ffn_kernel(xp_ref, xs_ref, g_ref, win_ref, wout_ref, op_ref, os_ref, *, np_t):
    i = pl.program_id(0)
    x = _load_rows(i, np_t, xp_ref, xs_ref)
    _store_rows(i, np_t, _swiglu(x, g_ref[...], win_ref, wout_ref), op_ref, os_ref)


def _ffn_call(xp, xs, g, w_in, w_out, *, alias):
    d = xp.shape[1]
    assert w_out.shape[0] % MXU_DIM == 0
    np_t, ns_t = xp.shape[0] // ROW_TILE, xs.shape[0] // ROW_TILE
    p_spec, s_spec = _split_specs(np_t, d)
    return pl.pallas_call(
        functools.partial(_ffn_kernel, np_t=np_t),
        grid=(np_t + ns_t,),
        in_specs=[p_spec, s_spec, _resident((1, d)), _wspec(w_in), _wspec(w_out)],
        out_specs=[p_spec, s_spec],
        out_shape=[jax.ShapeDtypeStruct(xp.shape, F32), jax.ShapeDtypeStruct(xs.shape, F32)],
        input_output_aliases={0: 0, 1: 1} if alias else {},
        compiler_params=_params(("arbitrary",)),
        name="ffn",
    )(xp, xs, g, _warg(w_in), _warg(w_out))


def _qkv_kernel(xp_ref, xs_ref, g_ref, w_ref, gq_ref, gk_ref, gmat_ref, kt_all_ref, vt_all_ref,
                q_ref, k_ref, v_ref, kt_ref, vt_ref, ks_ref, vs_ref, *, np_t, tiles_per_batch):
    del kt_all_ref, vt_all_ref
    i = pl.program_id(0)
    d = xp_ref.shape[1]
    h = _rms(_load_rows(i, np_t, xp_ref, xs_ref), g_ref[...]).astype(BF16)
    qkv = _dot(h, w_ref[...])
    gmat = gmat_ref[...]

    def head_norm(t, g):
        sq = (t * t).astype(BF16)
        ms = jnp.concatenate([_dot(sq[:, j * MXU_DIM:(j + 1) * MXU_DIM], gmat)
                              for j in range(d // MXU_DIM)], axis=1)
        return t * lax.rsqrt(ms + EPS) * g

    qn = head_norm(qkv[:, :d], gq_ref[...])
    kn = head_norm(qkv[:, d:2 * d], gk_ref[...])
    v = qkv[:, 2 * d:]
    for p in range(d // LANES):
        sl = slice(p * LANES, (p + 1) * LANES)
        q_ref[p] = qn[:, sl].astype(BF16)
        k_ref[p] = kn[:, sl].astype(BF16)
        v_ref[p] = v[:, sl].astype(BF16)

    @pl.when(jnp.logical_and(i < np_t, i % tiles_per_batch == tiles_per_batch - 1))
    def _():
        kt_ref[...] = kn.T
        vt_ref[...] = v.T

    @pl.when(i >= np_t)
    def _():
        ks_ref[...] = kn
        vs_ref[...] = v


def _qkv_call(xp, xs, g, w, gq, gk, gmat, kt_all, vt_all, layer, *, n_batch, tiles_per_batch):
    d = xp.shape[1]
    np_t, ns_t = xp.shape[0] // ROW_TILE, xs.shape[0] // ROW_TILE
    n_rows = xp.shape[0] + xs.shape[0]
    n_pairs = d // LANES
    p_spec, s_spec = _split_specs(np_t, d)
    pm_spec = pl.BlockSpec((n_pairs, ROW_TILE, LANES), lambda i: (0, i, 0))
    pm_shape = jax.ShapeDtypeStruct((n_pairs, n_rows, LANES), BF16)
    t_spec = pl.BlockSpec((None, None, d, ROW_TILE),
                          lambda i: (layer, jnp.minimum(i // tiles_per_batch, n_batch - 1), 0, 0))
    t_shape = jax.ShapeDtypeStruct(kt_all.shape, F32)
    any_spec = pl.BlockSpec(memory_space=pl.ANY)
    return pl.pallas_call(
        functools.partial(_qkv_kernel, np_t=np_t, tiles_per_batch=tiles_per_batch),
        grid=(np_t + ns_t,),
        in_specs=[p_spec, s_spec, _resident((1, d)), _wspec(w), _resident((1, d)), _resident((1, d)),
                  _resident(gmat.shape), any_spec, any_spec],
        out_specs=[pm_spec, pm_spec, pm_spec, t_spec, t_spec, s_spec, s_spec],
        out_shape=[pm_shape, pm_shape, pm_shape, t_shape, t_shape,
                   jax.ShapeDtypeStruct(xs.shape, F32), jax.ShapeDtypeStruct(xs.shape, F32)],
        input_output_aliases={7: 3, 8: 4},
        compiler_params=_params(("arbitrary",)),
        name="qkv",
    )(xp, xs, g, _warg(w), gq, gk, gmat, kt_all, vt_all)


def _bias_kernel(r_ref, o_ref):
    n_keys = o_ref.shape[2]
    width = r_ref.shape[2]
    x = jnp.broadcast_to(r_ref[0], (Q_TILE, width))
    row = lax.broadcasted_iota(jnp.int32, (Q_TILE, width), 0)
    shift = 1
    while shift < Q_TILE:
        x = jnp.where((row & shift) != 0, pltpu.roll(x, shift, 1), x)
        shift *= 2
    x = x[:, :n_keys]
    qc = lax.broadcasted_iota(jnp.int32, (Q_TILE, n_keys), 0) // CHUNK
    kc = lax.broadcasted_iota(jnp.int32, (Q_TILE, n_keys), 1) // CHUNK
    in_band = jnp.logical_and(kc >= qc, kc <= qc + PAST_CHUNKS)
    o_ref[0] = jnp.where(in_band, x, NEG_INF)


def _bias_call(rel_table):
    n_heads, n_rel = rel_table.shape
    clip = (n_rel - 1) // 2
    n_keys = Q_TILE + BAND_PAST
    width = 1024
    assert width >= n_keys + Q_TILE and BAND_PAST >= clip
    far = jnp.broadcast_to(rel_table[:, n_rel - 1:], (n_heads, BAND_PAST - clip))
    near = jnp.broadcast_to(rel_table[:, :1], (n_heads, n_keys - (BAND_PAST + clip) - 1))
    wrap = jnp.broadcast_to(rel_table[:, n_rel - 1:], (n_heads, width - n_keys))
    gen = jnp.concatenate([far, rel_table[:, ::-1], near, wrap], axis=1).reshape(n_heads, 1, width)
    return pl.pallas_call(
        _bias_kernel,
        grid=(n_heads,),
        in_specs=[pl.BlockSpec((1, 1, width), lambda h: (h, 0, 0))],
        out_specs=pl.BlockSpec((1, Q_TILE, n_keys), lambda h: (h, 0, 0)),
        out_shape=jax.ShapeDtypeStruct((n_heads, Q_TILE, n_keys), F32),
        compiler_params=_params(("arbitrary",)),
        name="rel_bias",
    )(gen)


def _split_heads(qp, lo):
    zero = jnp.zeros_like(qp)
    return jnp.concatenate([jnp.where(lo, qp, zero), jnp.where(lo, zero, qp)], axis=0)


def _attn_kernel(q_ref, kp_ref, kc_ref, vp_ref, vc_ref, bias_ref, x_ref, wo_ref, o_ref, opair_ref):
    t = pl.program_id(1)
    n_pairs = q_ref.shape[0]
    blocks_per_tile = ATTN_TILE // Q_TILE
    pen_prev = jnp.where(t < 1, NEG_INF, 0.0).astype(F32)
    lo = lax.broadcasted_iota(jnp.int32, (Q_TILE, LANES), 1) < LANES // 2

    for g in range(blocks_per_tile):
        rows = slice(g * Q_TILE, (g + 1) * Q_TILE)
        for p in range(n_pairs):
            qs = _split_heads(q_ref[p, rows, :], lo)
            s, pens, vals = [], [], []
            for j in range(BAND_PAST // Q_TILE + 1):
                blk = g + j
                from_prev = blk < blocks_per_tile
                k_ref, v_ref = (kp_ref, vp_ref) if from_prev else (kc_ref, vc_ref)
                krows = slice((blk % blocks_per_tile) * Q_TILE, (blk % blocks_per_tile + 1) * Q_TILE)
                s.append(_dot_nt(qs, k_ref[p, krows, :]) + bias_ref[p, :, j * Q_TILE:(j + 1) * Q_TILE])
                pens.append(pen_prev if from_prev else None)
                vals.append(v_ref[p, krows, :])
            mx = None
            for sj, pen in zip(s, pens):
                fj = _fold(jnp.maximum, sj)
                fj = fj if pen is None else fj + pen
                mx = fj if mx is None else jnp.maximum(mx, fj)
            m = jnp.max(mx, axis=1, keepdims=True)
            e = [jnp.exp(sj - (m if pen is None else m - pen)) for sj, pen in zip(s, pens)]
            denom = jnp.sum(sum(_fold(jnp.add, ej) for ej in e), axis=1, keepdims=True)
            o = sum(_dot(ej.astype(BF16), vj) for ej, vj in zip(e, vals))
            o = o * (1.0 / denom)
            opair_ref[p, rows, :] = jnp.where(lo, o[:Q_TILE], o[Q_TILE:]).astype(BF16)
    o_all = jnp.concatenate([opair_ref[p] for p in range(n_pairs)], axis=1)
    o_ref[...] = x_ref[...] + _dot(o_all, wo_ref[...])


def _attn_call(xp, q, k, v, bias2, w_o, *, n_batch, seq, alias):
    d = xp.shape[1]
    n_pairs = q.shape[0]
    nt = seq // ATTN_TILE
    assert ATTN_TILE == BAND_PAST and ATTN_TILE % Q_TILE == 0 and seq % ATTN_TILE == 0

    def blk(back):
        return pl.BlockSpec((n_pairs, ATTN_TILE, LANES),
                            lambda b, t: (0, b * nt + jnp.maximum(t - back, 0), 0))

    row_spec = pl.BlockSpec((ATTN_TILE, d), lambda b, t: (b * nt + t, 0))
    return pl.pallas_call(
        _attn_kernel,
        grid=(n_batch, nt),
        in_specs=[blk(0), blk(1), blk(0), blk(1), blk(0),
                  _wspec(bias2), row_spec, _wspec(w_o)],
        out_specs=row_spec,
        out_shape=jax.ShapeDtypeStruct(xp.shape, F32),
        scratch_shapes=[pltpu.VMEM((n_pairs, ATTN_TILE, LANES), BF16)],
        input_output_aliases={6: 0} if alias else {},
        compiler_params=_params(("arbitrary", "arbitrary")),
        name="attn_prompt",
    )(q, k, k, v, v, _warg(bias2), xp, _warg(w_o))


def _attn_sample_kernel(q_ref, k_ref, v_ref, ckt_ref, cvt_ref, bias_ref, x_ref, wo_ref, o_ref):
    n_pairs, n_new, _ = q_ref.shape
    n_past = ckt_ref.shape[1]
    lo = lax.broadcasted_iota(jnp.int32, (n_new, LANES), 1) < LANES // 2
    outs = []
    for p in range(n_pairs):
        sl = slice(p * LANES, (p + 1) * LANES)
        qs = _split_heads(q_ref[p], lo)
        bias = jnp.concatenate([bias_ref[p, 0:n_new, :], bias_ref[p, Q_TILE:Q_TILE + n_new, :]], axis=0)
        sc = _dot(qs, ckt_ref[sl, :].astype(BF16)) + bias[:, 0:n_past]
        sn = _dot_nt(qs, k_ref[p]) + bias[:, n_past:n_past + n_new]
        m = jnp.maximum(jnp.max(sc, axis=1, keepdims=True), jnp.max(sn, axis=1, keepdims=True))
        ec = jnp.exp(sc - m)
        en = jnp.exp(sn - m)
        denom = jnp.sum(ec, axis=1, keepdims=True) + jnp.sum(en, axis=1, keepdims=True)
        o = _dot_nt(ec.astype(BF16), cvt_ref[sl, :].astype(BF16)) + _dot(en.astype(BF16), v_ref[p])
        o = o * (1.0 / denom)
        outs.append(jnp.where(lo, o[:n_new], o[n_new:]).astype(BF16))
    o_all = jnp.concatenate(outs, axis=1)
    o_ref[...] = x_ref[...] + _dot(o_all, wo_ref[...])


def _attn_sample_call(xs, q, k, v, cache_kt, cache_vt, layer, bias2, w_o, *, n_dec, n_new, row0, alias):
    d = xs.shape[1]
    n_pairs = q.shape[0]
    n_past = cache_kt.shape[3]
    blk0 = row0 // n_new
    pm_spec = pl.BlockSpec((n_pairs, n_new, LANES), lambda b: (0, blk0 + b, 0))
    cache_spec = pl.BlockSpec((None, None, d, n_past), lambda b: (layer, b, 0, 0))
    row_spec = pl.BlockSpec((n_new, d), lambda b: (b, 0))
    return pl.pallas_call(
        _attn_sample_kernel,
        grid=(n_dec,),
        in_specs=[pm_spec, pm_spec, pm_spec, cache_spec, cache_spec,
                  _wspec(bias2), row_spec, _wspec(w_o)],
        out_specs=row_spec,
        out_shape=jax.ShapeDtypeStruct(xs.shape, F32),
        input_output_aliases={6: 0} if alias else {},
        compiler_params=_params(("arbitrary",)),
        name="attn_sample",
    )(q, k, v, cache_kt, cache_vt, _warg(bias2), xs, _warg(w_o))


def _glu(x, g, w_ref, b_ref):
    d = x.shape[1]
    a = _dot(_rms(x, g).astype(BF16), w_ref[...]) + b_ref[...]
    return a[:, :d] * _sigmoid(a[:, d:])


def _fill_ctx(ctx_ref, halo, u):
    for c in range(ctx_ref.shape[0]):
        sl = slice(c * LANES, (c + 1) * LANES)
        ctx_ref[c, 0:HALO_ROWS, :] = halo[:, sl]
        ctx_ref[c, HALO_ROWS:, :] = u[:, sl]


def _depthwise_block(ctx_ref, y_ref, dw_ref, dwb_ref, r0, rb, slabs=None):
    conv_w = dw_ref.shape[0]
    first = HALO_ROWS - (conv_w - 1)
    per_phase = rb // CONV_ROW_STRIDE
    for c in (range(y_ref.shape[0]) if slabs is None else slabs):
        sl = slice(c * LANES, (c + 1) * LANES)
        win = ctx_ref.at[c, pl.ds(r0, rb + HALO_ROWS), :]
        out = y_ref.at[c, pl.ds(r0, rb), :]
        for ph in range(CONV_ROW_STRIDE):
            acc = jnp.zeros((per_phase, LANES), F32)
            for k in range(conv_w):
                acc = acc + dw_ref[k:k + 1, sl] * win[pl.ds(first + k + ph, per_phase, stride=CONV_ROW_STRIDE), :]
            out[pl.ds(ph, per_phase, stride=CONV_ROW_STRIDE), :] = acc + dwb_ref[:, sl]


def _depthwise_steps(ctx_ref, y_ref, dw_ref, dwb_ref):
    n_slabs, n_rows, _ = y_ref.shape
    rb = min(CONV_ROW_BLOCK, n_rows)
    groups = [range(g, g + CONV_STEP_SLABS) for g in range(0, n_slabs, CONV_STEP_SLABS)]
    return [functools.partial(_depthwise_block, ctx_ref, y_ref, dw_ref, dwb_ref, i * rb, rb, slabs)
            for i in range(n_rows // rb) for slabs in groups]


def _emit_interleaved(*stages):
    for k in range(max(len(steps) for steps in stages)):
        for steps in stages:
            if k < len(steps):
                steps[k]()


def _depthwise(ctx_ref, y_ref, dw_ref, dwb_ref):
    n_rows = y_ref.shape[1]
    rb = min(CONV_ROW_BLOCK, n_rows)

    def body(i, carry):
        _depthwise_block(ctx_ref, y_ref, dw_ref, dwb_ref, pl.multiple_of(i * rb, rb), rb)
        return carry

    lax.fori_loop(0, n_rows // rb, body, 0)


def _conv_post(y_ref, x, lng_ref, lnb_ref, w2_ref, b2_ref):
    y = jnp.concatenate([y_ref[c] for c in range(y_ref.shape[0])], axis=1)
    yc = y - jnp.mean(y, axis=-1, keepdims=True)
    yn = yc * lax.rsqrt(jnp.mean(yc * yc, axis=-1, keepdims=True) + EPS) * lng_ref[...] + lnb_ref[...]
    act = (yn * _sigmoid(yn)).astype(BF16)
    return x + _dot(act, w2_ref[...]) + b2_ref[...]


def _conv_scratch(n_rows, d):
    return [pltpu.VMEM((d // LANES, HALO_ROWS + n_rows, LANES), F32), pltpu.VMEM((d // LANES, n_rows, LANES), F32)]


def _convffn_kernel(x_ref, gmix_ref, w1_ref, b1_ref, dw_ref, dwb_ref, lng_ref, lnb_ref, w2_ref, b2_ref,
                    gffn_ref, win_ref, wout_ref, o_ref, last_ref,
                    ctx_ref, y_ref, carry_ref, xkeep_ref, mid_ref, *, tiles_per_batch, n_tiles):
    s = pl.program_id(0)
    slot = s % 2

    @pl.when(s == 0)
    def _():
        ctx_ref[...] = jnp.zeros_like(ctx_ref)
        xkeep_ref[...] = jnp.zeros_like(xkeep_ref)
        mid_ref[...] = jnp.zeros_like(mid_ref)
        carry_ref[...] = jnp.zeros_like(carry_ref)

    d = x_ref.shape[1]
    d_ff = wout_ref.shape[0]
    st = {}

    def ffn_begin():
        st["xb"] = mid_ref[1 - slot]
        st["hb"] = _rms(st["xb"], gffn_ref[...]).astype(BF16)
        st["acc"] = st["xb"]

    def ffn_chunk(lo):
        hi = lo + MXU_DIM
        gate = _dot(st["hb"], win_ref[:, lo:hi])
        up = _dot(st["hb"], win_ref[:, d_ff + lo:d_ff + hi])
        a = (gate * _sigmoid(gate) * up).astype(BF16)
        st["acc"] = st["acc"] + _dot(a, wout_ref[lo:hi, :])

    def ffn_end():
        o_ref[...] = st["acc"]

    prev_ctx = ctx_ref.at[1 - slot]

    def conv_post():
        mid_ref[slot] = _conv_post(y_ref, xkeep_ref[1 - slot], lng_ref, lnb_ref, w2_ref, b2_ref)

    cur_ctx = ctx_ref.at[slot]

    def glu_begin():
        st["x"] = x_ref[...]
        st["h"] = _rms(st["x"], gmix_ref[...]).astype(BF16)
        xkeep_ref[slot] = st["x"]

    def glu_chunk(lo):
        hi = lo + MXU_DIM
        val = _dot(st["h"], w1_ref[:, lo:hi]) + b1_ref[:, lo:hi]
        gate = _dot(st["h"], w1_ref[:, d + lo:d + hi]) + b1_ref[:, d + lo:d + hi]
        u = val * _sigmoid(gate)
        halo = carry_ref[:, lo:hi]
        halo = jnp.where(s % tiles_per_batch == 0, jnp.zeros_like(halo), halo)
        for j in range(MXU_DIM // LANES):
            c = lo // LANES + j
            cur_ctx[c, 0:HALO_ROWS, :] = halo[:, j * LANES:(j + 1) * LANES]
            cur_ctx[c, HALO_ROWS:, :] = u[:, j * LANES:(j + 1) * LANES]
        carry_ref[:, lo:hi] = u[ROW_TILE - HALO_ROWS:, :]

    ffn_steps = [ffn_begin] + [functools.partial(ffn_chunk, lo) for lo in range(0, d_ff, MXU_DIM)] + [ffn_end]
    conv_steps = _depthwise_steps(prev_ctx, y_ref, dw_ref, dwb_ref) + [conv_post]
    glu_steps = [glu_begin] + [functools.partial(glu_chunk, lo) for lo in range(0, d, MXU_DIM)]
    _emit_interleaved(ffn_steps, conv_steps, glu_steps)

    @pl.when(jnp.logical_and(s < n_tiles, s % tiles_per_batch == tiles_per_batch - 1))
    def _():
        last_ref[...] = carry_ref[...]


def _convffn_call(xp, gmix, w1, b1, dw, dwb, lng, lnb, w2, b2, gffn, w_in, w_out, *, n_batch, tiles_per_batch):
    d = xp.shape[1]
    n_tiles = n_batch * tiles_per_batch
    assert xp.shape[0] == n_tiles * ROW_TILE and w_out.shape[0] % MXU_DIM == 0
    vec = _resident((1, d))
    ctx, y = _conv_scratch(ROW_TILE, d)
    return pl.pallas_call(
        functools.partial(_convffn_kernel, tiles_per_batch=tiles_per_batch, n_tiles=n_tiles),
        grid=(n_tiles + 2,),
        in_specs=[pl.BlockSpec((ROW_TILE, d), lambda s: (jnp.minimum(s, n_tiles - 1), 0)),
                  vec, _wspec(w1), _resident((1, 2 * d)), _resident(dw.shape), vec, vec, vec,
                  _wspec(w2), vec, vec, _wspec(w_in), _wspec(w_out)],
        out_specs=[pl.BlockSpec((ROW_TILE, d), lambda s: (jnp.maximum(s - 2, 0), 0)),
                   pl.BlockSpec((HALO_ROWS, d), lambda s: (jnp.minimum(s, n_tiles - 1) // tiles_per_batch, 0))],
        out_shape=[jax.ShapeDtypeStruct(xp.shape, F32), jax.ShapeDtypeStruct((n_batch * HALO_ROWS, d), F32)],
        scratch_shapes=[pltpu.VMEM((2,) + ctx.shape, F32), y, pltpu.VMEM((HALO_ROWS, d), F32),
                        pltpu.VMEM((2, ROW_TILE, d), F32), pltpu.VMEM((2, ROW_TILE, d), F32)],
        input_output_aliases={0: 0},
        compiler_params=_params(("arbitrary",)),
        name="conv_ffn_prompt",
    )(xp, gmix, _warg(w1), b1, dw, dwb, lng, lnb, _warg(w2), b2, gffn, _warg(w_in), _warg(w_out))


def _glu_kernel(x_ref, g_ref, w_ref, b_ref, u_ref):
    u_ref[...] = _glu(x_ref[...], g_ref[...], w_ref, b_ref)


def _glu_call(x, g, w, b):
    d = x.shape[1]
    row_spec = pl.BlockSpec((ROW_TILE, d), lambda i: (i, 0))
    return pl.pallas_call(
        _glu_kernel,
        grid=(x.shape[0] // ROW_TILE,),
        in_specs=[row_spec, _resident((1, d)), _wspec(w), _resident((1, 2 * d))],
        out_specs=row_spec,
        out_shape=jax.ShapeDtypeStruct(x.shape, F32),
        compiler_params=_params(("arbitrary",)),
        name="conv_glu_sample",
    )(x, g, _warg(w), b)


def _conv_sample_kernel(state_ref, u_ref, x_ref, dw_ref, dwb_ref, lng_ref, lnb_ref, w2_ref, b2_ref, o_ref,
                        ctx_ref, y_ref):
    _fill_ctx(ctx_ref, state_ref[...], u_ref[...])
    _depthwise(ctx_ref, y_ref, dw_ref, dwb_ref)
    o_ref[...] = _conv_post(y_ref, x_ref[...], lng_ref, lnb_ref, w2_ref, b2_ref)


def _conv_sample_call(xs, u, state, layer, dw, dwb, lng, lnb, w2, b2, *, n_dec, n_new):
    d = xs.shape[1]
    row_spec = pl.BlockSpec((n_new, d), lambda b: (b, 0))
    state_spec = pl.BlockSpec((None, None, HALO_ROWS, d), lambda b: (layer, b, 0, 0))
    return pl.pallas_call(
        _conv_sample_kernel,
        grid=(n_dec,),
        in_specs=[state_spec, row_spec, row_spec, _resident(dw.shape), _resident((1, d)), _resident((1, d)),
                  _resident((1, d)), _wspec(w2), _resident((1, d))],
        out_specs=row_spec,
        out_shape=jax.ShapeDtypeStruct(xs.shape, F32),
        scratch_shapes=_conv_scratch(n_new, d),
        input_output_aliases={2: 0},
        compiler_params=_params(("arbitrary",)),
        name="conv_sample",
    )(state, u, xs, dw, dwb, lng, lnb, _warg(w2), b2)


def _ffn_rows_kernel(x_ref, g_ref, win_ref, wout_ref, o_ref):
    o_ref[...] = _swiglu(x_ref[...], g_ref[...], win_ref, wout_ref)


def _ffn_rows_call(x, g, w_in, w_out):
    d = x.shape[1]
    row_spec = pl.BlockSpec((ROW_TILE, d), lambda i: (i, 0))
    return pl.pallas_call(
        _ffn_rows_kernel,
        grid=(x.shape[0] // ROW_TILE,),
        in_specs=[row_spec, _resident((1, d)), _wspec(w_in), _wspec(w_out)],
        out_specs=row_spec,
        out_shape=jax.ShapeDtypeStruct(x.shape, F32),
        input_output_aliases={0: 0},
        compiler_params=_params(("arbitrary",)),
        name="ffn_sample",
    )(x, g, _warg(w_in), _warg(w_out))


def kernel(x_prompt, x_sample, cache_k, cache_v, state_conv, norm_mix, norm_ffn, w_qkv, q_norm, k_norm,
           rel_table, w_o, pw1_w, pw1_b, dw_w, dw_b, conv_ln_g, conv_ln_b, pw2_w, pw2_b, ffn_w_in, ffn_w_out):
    n_batch, seq, d = x_prompt.shape
    n_dec, n_new, _ = x_sample.shape
    depth = norm_mix.shape[0]
    head_dim = q_norm.shape[1]
    n_heads = d // head_dim
    conv_w = dw_w.shape[1]
    n_prompt = n_batch * seq
    n_sample = n_dec * n_new
    keep = min(BAND_PAST, seq)
    tiles_per_batch = seq // ROW_TILE
    assert seq % ROW_TILE == 0 and n_sample % ROW_TILE == 0 and keep == ROW_TILE
    assert LANES == 2 * head_dim and d % MXU_DIM == 0 and n_new % 8 == 0
    assert conv_w - 1 <= HALO_ROWS and n_new >= conv_w - 1
    assert cache_k.shape[2] == BAND_PAST and cache_k.shape[1] == n_dec

    xp = x_prompt.reshape(n_prompt, d)
    xs = x_sample.reshape(n_sample, d)

    blk = jnp.arange(MXU_DIM) // head_dim
    gmat = jnp.where(blk[:, None] == blk[None, :], 1.0 / head_dim, 0.0).astype(BF16)
    inv_sqrt_dh = float(head_dim) ** -0.5
    cache_kt = jnp.transpose(cache_k, (0, 1, 3, 4, 2)).reshape(cache_k.shape[0], n_dec, d, BAND_PAST)
    cache_vt = jnp.transpose(cache_v, (0, 1, 3, 4, 2)).reshape(cache_v.shape[0], n_dec, d, BAND_PAST)
    state_pad = jnp.pad(state_conv, ((0, 0), (0, 0), (HALO_ROWS - (conv_w - 1), 0), (0, 0)))

    def heads_last(t):
        return jnp.transpose(t.reshape(t.shape[0], t.shape[1], n_heads, head_dim, t.shape[3]), (0, 1, 4, 2, 3))

    ffn_in_bf, ffn_out_bf, qkv_bf, wo_bf, pw1_bf, pw2_bf = (
        t.astype(BF16) for t in (ffn_w_in, ffn_w_out, w_qkv, w_o, pw1_w, pw2_w))
    n_attn = rel_table.shape[0]
    bias_all = _bias_call(rel_table.reshape(n_attn * n_heads, rel_table.shape[2])).reshape(
        n_attn, n_heads // 2, 2 * Q_TILE, Q_TILE + BAND_PAST)

    kt_all = jnp.zeros((n_attn, n_batch, d, keep), F32)
    vt_all = jnp.zeros((n_attn, n_batch, d, keep), F32)
    ks_new, vs_new, cp_new, cs_new = [], [], [], []
    for layer in range(depth):
        g_mix = norm_mix[layer].reshape(1, d)
        g_ffn = norm_ffn[layer].reshape(1, d)
        w_in, w_out = _Layer(ffn_in_bf, layer), _Layer(ffn_out_bf, layer)
        own = layer > 0
        if layer % 2 == 0:
            a = layer // 2
            gq = (jnp.tile(q_norm[a], n_heads) * inv_sqrt_dh).reshape(1, d)
            gk = jnp.tile(k_norm[a], n_heads).reshape(1, d)
            q, k, v, kt_all, vt_all, ksf, vsf = _qkv_call(xp, xs, g_mix, _Layer(qkv_bf, a), gq, gk, gmat,
                                                          kt_all, vt_all, a,
                                                          n_batch=n_batch, tiles_per_batch=tiles_per_batch)
            bias2 = _Layer(bias_all, a)
            wo = _Layer(wo_bf, a)
            xp = _attn_call(xp, q, k, v, bias2, wo, n_batch=n_batch, seq=seq, alias=own)
            xs = _attn_sample_call(xs, q, k, v, cache_kt, cache_vt, a, bias2, wo,
                                   n_dec=n_dec, n_new=n_new, row0=n_prompt, alias=own)
            ks_new.append(ksf.reshape(n_dec, n_new, n_heads, head_dim))
            vs_new.append(vsf.reshape(n_dec, n_new, n_heads, head_dim))
            xp, xs = _ffn_call(xp, xs, g_ffn, w_in, w_out, alias=True)
        else:
            c = layer // 2
            glu_args = (g_mix, _Layer(pw1_bf, c), pw1_b[c].reshape(1, 2 * d))
            conv_args = (dw_w[c], dw_b[c].reshape(1, d), conv_ln_g[c].reshape(1, d), conv_ln_b[c].reshape(1, d),
                         _Layer(pw2_bf, c), pw2_b[c].reshape(1, d))
            xp, u_last = _convffn_call(xp, *glu_args, *conv_args, g_ffn, w_in, w_out,
                                       n_batch=n_batch, tiles_per_batch=tiles_per_batch)
            u_s = _glu_call(xs, *glu_args)
            xs = _conv_sample_call(xs, u_s, state_pad, c, *conv_args, n_dec=n_dec, n_new=n_new)
            xs = _ffn_rows_call(xs, g_ffn, w_in, w_out)
            cp_new.append(u_last.reshape(n_batch, HALO_ROWS, d)[:, HALO_ROWS - (conv_w - 1):])
            cs_new.append(u_s.reshape(n_dec, n_new, d)[:, n_new - (conv_w - 1):])

    return (xp.reshape(n_batch, seq, d), xs.reshape(n_dec, n_new, d),
            heads_last(kt_all), heads_last(vt_all), jnp.stack(ks_new), jnp.stack(vs_new),
            jnp.stack(cp_new), jnp.stack(cs_new))
```
